```python
import jax, jax.numpy as jnp
from jax import lax
import numpy as np

D_MODEL = 1024
BATCH = 4
SEQ = 4096
DEPTH = 4

D_MIX = D_MODEL
D_POOL = D_MIX // 2
D_HGRN = D_MIX - D_POOL
POOL_WINDOWS = (2, 4, 8, 16)
N_POOL_GROUPS = len(POOL_WINDOWS)
POOL_GROUP_DIM = D_POOL // N_POOL_GROUPS
HGRN_HEADS = 4
HGRN_HEAD_DIM = D_HGRN // HGRN_HEADS
CHUNK = 64
D_IN = D_POOL + 4 * D_HGRN
D_FF = ((8 * D_MODEL // 3 + 127) // 128) * 128
ALPHA = (2.0 * DEPTH) ** 0.25
BETA = (8.0 * DEPTH) ** -0.25
LN_EPS = 1e-5
RMS_EPS = 1e-6

kernel_name = "macaron_pool_hgrn2_deepnorm_hybrid"


def layer_norm(x, g, b):
    x32 = x.astype(jnp.float32)
    mu = jnp.mean(x32, axis=-1, keepdims=True)
    var = jnp.mean(jnp.square(x32 - mu), axis=-1, keepdims=True)
    y = (x32 - mu) * lax.rsqrt(var + LN_EPS) * g.astype(jnp.float32) + b.astype(jnp.float32)
    return y.astype(x.dtype)


def swiglu(x, w_gate, w_up, w_down):
    return (jax.nn.silu(x @ w_gate) * (x @ w_up)) @ w_down


def multiscale_pool(u, pool_w, pool_scale):
    B, S, _ = u.shape
    u32 = u.astype(jnp.float32).reshape(B, S, N_POOL_GROUPS, POOL_GROUP_DIM)
    c = jnp.cumsum(u32, axis=1)
    pos = jnp.arange(S)
    means = []
    for g, w in enumerate(POOL_WINDOWS):
        cg = c[:, :, g]
        lag = jnp.pad(cg, ((0, 0), (w, 0), (0, 0)))[:, :S]
        cnt = jnp.minimum(pos + 1, w).astype(jnp.float32)[None, :, None]
        means.append((cg - lag) / cnt)
    pooled = jnp.stack(means, axis=2) - u32
    y = jnp.einsum('bsgc,gcd->bsgd', pooled.astype(u.dtype), pool_w)
    return y.reshape(B, S, D_POOL) * pool_scale


def hgrn2_recurrence(q, f_raw, v, lb):
    B, S, _ = q.shape
    n = S // CHUNK
    log_f = jnp.logaddexp(jnp.log(lb), jnp.log1p(-lb) + jax.nn.log_sigmoid(f_raw))
    k = (1.0 - lb) * jax.nn.sigmoid(-f_raw)
    q = q * (HGRN_HEAD_DIM ** -0.5)

    def heads(t):
        return t.reshape(B, n, CHUNK, HGRN_HEADS, HGRN_HEAD_DIM).transpose(1, 0, 3, 2, 4)

    causal = jnp.tril(jnp.ones((CHUNK, CHUNK), dtype=bool))[:, :, None]

    def step(state, inp):
        q_c, k_c, v_c, g_c = inp
        b = jnp.cumsum(g_c, axis=2)
        o_inter = jnp.einsum('bhtd,bhdv->bhtv', q_c * jnp.exp(b), state)
        rel = b[:, :, :, None, :] - b[:, :, None, :, :]
        decay = jnp.exp(jnp.where(causal, rel, -jnp.inf))
        scores = jnp.einsum('bhtd,bhsd,bhtsd->bhts', q_c, k_c, decay)
        o_intra = jnp.einsum('bhts,bhsv->bhtv', scores, v_c)
        b_last = b[:, :, -1:, :]
        k_dec = k_c * jnp.exp(b_last - b)
        new_state = jnp.exp(b_last[:, :, 0, :])[..., None] * state + jnp.einsum('bhsd,bhsv->bhdv', k_dec, v_c)
        return new_state, o_inter + o_intra

    s0 = jnp.zeros((B, HGRN_HEADS, HGRN_HEAD_DIM, HGRN_HEAD_DIM), jnp.float32)
    _, o = lax.scan(step, s0, (heads(q), heads(k), heads(v), heads(log_f)))
    return o.transpose(1, 0, 3, 2, 4).reshape(B, S, D_HGRN)


def head_rms_norm(o, g):
    B, S, _ = o.shape
    o = o.reshape(B, S, HGRN_HEADS, HGRN_HEAD_DIM)
    o = o * lax.rsqrt(jnp.mean(jnp.square(o), axis=-1, keepdims=True) + RMS_EPS) * g.astype(jnp.float32)
    return o.reshape(B, S, D_HGRN)


def setup_inputs(seed: int = 0) -> dict:
    key = jax.random.key(seed)
    ks = jax.random.split(key, 16)
    nrm = jax.random.normal
    f32 = jnp.float32
    x = nrm(ks[0], (BATCH, SEQ, D_MODEL), f32)
    w_in = nrm(ks[1], (DEPTH, D_MODEL, D_IN), f32) * D_MODEL ** -0.5
    pool_w = nrm(ks[2], (DEPTH, N_POOL_GROUPS, POOL_GROUP_DIM, POOL_GROUP_DIM), f32) * POOL_GROUP_DIM ** -0.5
    pool_scale = 1.0 + 0.02 * nrm(ks[3], (DEPTH, D_POOL), f32)
    lb_param = 0.5 * nrm(ks[4], (DEPTH, D_HGRN), f32)
    hgrn_norm_g = 1.0 + 0.02 * nrm(ks[5], (DEPTH, HGRN_HEAD_DIM), f32)
    w_out = nrm(ks[6], (DEPTH, D_MIX, D_MODEL), f32) * (D_MIX ** -0.5 * BETA)
    ffn1_gate = nrm(ks[7], (DEPTH, D_MODEL, D_FF), f32) * D_MODEL ** -0.5
    ffn1_up = nrm(ks[8], (DEPTH, D_MODEL, D_FF), f32) * D_MODEL ** -0.5
    ffn1_down = nrm(ks[9], (DEPTH, D_FF, D_MODEL), f32) * (D_FF ** -0.5 * BETA)
    ffn2_gate = nrm(ks[10], (DEPTH, D_MODEL, D_FF), f32) * D_MODEL ** -0.5
    ffn2_up = nrm(ks[11], (DEPTH, D_MODEL, D_FF), f32) * D_MODEL ** -0.5
    ffn2_down = nrm(ks[12], (DEPTH, D_FF, D_MODEL), f32) * (D_FF ** -0.5 * BETA)
    ln_g = 1.0 + 0.02 * nrm(ks[13], (DEPTH, 3, D_MODEL), f32)
    ln_b = 0.02 * nrm(ks[14], (DEPTH, 3, D_MODEL), f32)
    return {"x": x, "w_in": w_in, "pool_w": pool_w, "pool_scale": pool_scale, "lb_param": lb_param,
            "hgrn_norm_g": hgrn_norm_g, "w_out": w_out, "ffn1_gate": ffn1_gate, "ffn1_up": ffn1_up,
            "ffn1_down": ffn1_down, "ffn2_gate": ffn2_gate, "ffn2_up": ffn2_up, "ffn2_down": ffn2_down,
            "ln_g": ln_g, "ln_b": ln_b}


def reference(x, w_in, pool_w, pool_scale, lb_param, hgrn_norm_g, w_out, ffn1_gate, ffn1_up, ffn1_down,
              ffn2_gate, ffn2_up, ffn2_down, ln_g, ln_b):
    f32 = jnp.float32
    lb_all = jnp.cumsum(jax.nn.softmax(lb_param.astype(f32), axis=0), axis=0)
    lb_all = lb_all - lb_all[0:1]
    for l in range(DEPTH):
        x = layer_norm(ALPHA * x + 0.5 * swiglu(x, ffn1_gate[l], ffn1_up[l], ffn1_down[l]), ln_g[l, 0], ln_b[l, 0])
        h = x @ w_in[l]
        u = h[..., :D_POOL]
        q, f_raw, v, g = jnp.split(h[..., D_POOL:], 4, axis=-1)
        y_pool = multiscale_pool(u, pool_w[l], pool_scale[l])
        o = hgrn2_recurrence(q.astype(f32), f_raw.astype(f32), v.astype(f32), lb_all[l])
        y_hgrn = head_rms_norm(o, hgrn_norm_g[l]) * jax.nn.silu(g.astype(f32))
        mix = jnp.concatenate([y_pool.astype(x.dtype), y_hgrn.astype(x.dtype)], axis=-1) @ w_out[l]
        x = layer_norm(ALPHA * x + mix, ln_g[l, 1], ln_b[l, 1])
        x = layer_norm(ALPHA * x + 0.5 * swiglu(x, ffn2_gate[l], ffn2_up[l], ffn2_down[l]), ln_g[l, 2], ln_b[l, 2])
    return x
```

```python
import functools
import math

import numpy as np
import jax
import jax.numpy as jnp
from jax import lax
from jax.experimental import pallas as pl
from jax.experimental.pallas import tpu as pltpu

F32 = jnp.float32
BF16 = jnp.bfloat16

SUBLANES = 8
LANES = 128

POOL_WINDOWS = (2, 4, 8, 16)
HGRN_HEADS = 4
LN_EPS = 1e-5
RMS_EPS = 1e-6

TILE = 256
SEG = TILE // SUBLANES
FFN_ROWS = 1024
FFN_COLS = 256
POOL_HALO = max(POOL_WINDOWS) - 1


def _layer_norm(y, g, b):
    mu = jnp.mean(y, axis=-1, keepdims=True)
    yc = y - mu
    var = jnp.mean(yc * yc, axis=-1, keepdims=True)
    return yc * lax.rsqrt(var + LN_EPS) * g + b


def _silu(x):
    return x * jax.nn.sigmoid(x)


def _ffn_kernel(x_ref, wg_ref, wu_ref, wd_ref, g_ref, b_ref, o_ref, xb_ref, acc_ref, *, alpha):
    j = pl.program_id(1)

    @pl.when(j == 0)
    def _():
        xb_ref[...] = x_ref[...].astype(BF16)
        acc_ref[...] = jnp.zeros_like(acc_ref)

    xb = xb_ref[...]
    gate = jnp.dot(xb, wg_ref[...], preferred_element_type=F32)
    up = jnp.dot(xb, wu_ref[...], preferred_element_type=F32)
    hidden = (_silu(gate) * up).astype(BF16)
    acc_ref[...] += jnp.dot(hidden, wd_ref[...], preferred_element_type=F32)

    @pl.when(j == pl.num_programs(1) - 1)
    def _():
        y = alpha * x_ref[...] + 0.5 * acc_ref[...]
        o_ref[...] = _layer_norm(y, g_ref[...], b_ref[...])


def _ffn(x2d, wg, wu, wd, g, b, *, alpha):
    rows, d = x2d.shape
    dff = wg.shape[1]
    assert rows % FFN_ROWS == 0 and dff % FFN_COLS == 0
    grid = (rows // FFN_ROWS, dff // FFN_COLS)
    vmem = (2 * 2 * FFN_ROWS * d * 4
            + FFN_ROWS * d * (2 + 4)
            + 2 * 3 * d * FFN_COLS * 2
            + 4 * FFN_ROWS * FFN_COLS * 4
            + FFN_ROWS * d * 4)
    return pl.pallas_call(
        functools.partial(_ffn_kernel, alpha=alpha),
        grid=grid,
        in_specs=[
            pl.BlockSpec((FFN_ROWS, d), lambda i, j: (i, 0)),
            pl.BlockSpec((d, FFN_COLS), lambda i, j: (0, j)),
            pl.BlockSpec((d, FFN_COLS), lambda i, j: (0, j)),
            pl.BlockSpec((FFN_COLS, d), lambda i, j: (j, 0)),
            pl.BlockSpec((1, d), lambda i, j: (0, 0)),
            pl.BlockSpec((1, d), lambda i, j: (0, 0)),
        ],
        out_specs=pl.BlockSpec((FFN_ROWS, d), lambda i, j: (i, 0)),
        out_shape=jax.ShapeDtypeStruct((rows, d), F32),
        scratch_shapes=[pltpu.VMEM((FFN_ROWS, d), BF16), pltpu.VMEM((FFN_ROWS, d), F32)],
        compiler_params=pltpu.CompilerParams(
            dimension_semantics=("parallel", "arbitrary"),
            vmem_limit_bytes=int(vmem * 1.25)),
        name="ffn",
    )(x2d, wg, wu, wd, g.reshape(1, d), b.reshape(1, d))


def _level_plan():
    plan = []
    m = SUBLANES
    while m >= 2:
        plan.append(("seg", m))
        m //= 2
    m = SEG
    while m >= 2:
        plan.append(("pos", m))
        m //= 2
    return plan


def _level_masks():
    r = np.arange(TILE)
    seg, pos = r % SUBLANES, r // SUBLANES
    masks = []
    for kind, m in _level_plan():
        h = m // 2
        if kind == "seg":
            same = (seg[:, None] // m) == (seg[None, :] // m)
            late, early = (seg % m) >= h, (seg % m) < h
        else:
            same = (seg[:, None] == seg[None, :]) & ((pos[:, None] // m) == (pos[None, :] // m))
            late, early = (pos % m) >= h, (pos % m) < h
        masks.append(same & late[:, None] & early[None, :])
    masks = np.stack(masks)
    time = seg * SEG + pos
    assert (masks.sum(0) == (time[None, :] < time[:, None])).all()
    return masks.astype(np.float32)


def _pool_group(ext, u_g, window, tile_idx):
    cur = ext[(POOL_HALO - (window - 1)) * SUBLANES:]
    span = 1
    while span < window:
        cur = cur[span * SUBLANES:] + cur[:-span * SUBLANES]
        span *= 2
    row = lax.broadcasted_iota(jnp.int32, (TILE, LANES), 0)
    pos = tile_idx * TILE + (row & (SUBLANES - 1)) * SEG + (row >> 3)
    cnt = jnp.minimum(pos + 1, window).astype(F32)
    return cur / cnt - u_g


def _log_decay(f_raw, lb):
    log_sig = jnp.minimum(f_raw, 0.0) - jnp.log1p(jnp.exp(-jnp.abs(f_raw)))
    a = jnp.log(lb)
    c = jnp.log1p(-lb) + log_sig
    hi = jnp.maximum(a, c)
    return hi + jnp.log1p(jnp.exp(-jnp.abs(a - c)))


def _vregs(a):
    return [a[j * SUBLANES:(j + 1) * SUBLANES] for j in range(a.shape[0] // SUBLANES)]


def _hgrn_head(q, f_raw, v, lb, state_t, masks_ref):
    lf = _log_decay(f_raw, lb)
    kk = (1.0 - lb) * jax.nn.sigmoid(-f_raw)
    qs = q * (LANES ** -0.5)

    lf_v = _vregs(lf)
    b_v = [lf_v[0]]
    for j in range(1, SEG):
        b_v.append(b_v[-1] + lf_v[j])
    b_loc = jnp.concatenate(b_v, axis=0)
    seg_tot = b_v[-1]

    sub = lax.broadcasted_iota(jnp.int32, (SUBLANES, LANES), 0)
    inc = seg_tot
    for sh in (1, 2, 4):
        inc = inc + jnp.where(sub >= sh, pltpu.roll(inc, sh, 0), 0.0)
    off = inc - seg_tot
    total = inc[SUBLANES - 1:SUBLANES, :]
    bg = b_loc + jnp.concatenate([off] * SEG, axis=0)

    q_in = (qs * jnp.exp(bg)).astype(BF16)
    o = lax.dot_general(q_in, state_t.astype(BF16), (((1,), (1,)), ((), ())),
                        preferred_element_type=F32)
    k_dec = (kk * jnp.exp(total - bg)).astype(BF16)
    v_b = v.astype(BF16)
    new_state_t = state_t * jnp.exp(total) + lax.dot_general(
        v_b, k_dec, (((0,), (0,)), ((), ())), preferred_element_type=F32)

    row_sub = lax.broadcasted_iota(jnp.int32, (TILE, LANES), 0) & (SUBLANES - 1)
    scores = jnp.zeros((TILE, TILE), F32)
    for lvl, (kind, m) in enumerate(_level_plan()):
        h = m // 2
        if kind == "seg":
            ref = off
            for blk in range(SUBLANES // m):
                src = blk * m + h
                in_blk = (sub >= blk * m) & (sub < (blk + 1) * m)
                ref = jnp.where(in_blk, jnp.broadcast_to(off[src:src + 1, :], off.shape), ref)
            t = b_loc + jnp.concatenate([off - ref] * SEG, axis=0)
            z = jnp.where((row_sub & (m - 1)) >= h, t, -t)
        else:
            z_v = []
            for j in range(SEG):
                ref_v = b_v[(j // m) * m + h - 1]
                z_v.append(b_v[j] - ref_v if j % m >= h else ref_v - b_v[j])
            z = jnp.concatenate(z_v, axis=0)
        x = jnp.exp(z)
        part = lax.dot_general((x * qs).astype(BF16), (x * kk).astype(BF16),
                               (((1,), (1,)), ((), ())), preferred_element_type=F32)
        scores = scores + masks_ref[lvl] * part

    o = o + jnp.dot(scores.astype(BF16), v_b, preferred_element_type=F32)
    o = o + jnp.sum(qs * kk, axis=-1, keepdims=True) * v
    return o, new_state_t


def _mixer_kernel(x_ref, win_ref, poolw_ref, pscale_ref, lb_ref, ng_ref, wout_ref, lng_ref, lnb_ref,
                  masks_ref, o_ref, state_ref, tail_ref, *, alpha):
    tile_idx = pl.program_id(1)
    d_pool = LANES * len(POOL_WINDOWS)
    d_hgrn = LANES * HGRN_HEADS
    halo_rows = POOL_HALO * SUBLANES

    @pl.when(tile_idx == 0)
    def _():
        state_ref[...] = jnp.zeros_like(state_ref)
        tail_ref[...] = jnp.zeros_like(tail_ref)

    x = x_ref[...]
    h = jnp.dot(x.astype(BF16), win_ref[...], preferred_element_type=F32)

    u = h[:, :d_pool]
    cur_tail = u[TILE - halo_rows:, :]
    sub = lax.broadcasted_iota(jnp.int32, cur_tail.shape, 0) & (SUBLANES - 1)
    head = jnp.where(sub == 0,
                     pltpu.roll(tail_ref[...], halo_rows - (SUBLANES - 1), 0),
                     pltpu.roll(cur_tail, 1, 0))
    tail_ref[...] = cur_tail
    ext = jnp.concatenate([head, u], axis=0)
    mixed = []
    for g, window in enumerate(POOL_WINDOWS):
        cols = slice(g * LANES, (g + 1) * LANES)
        pooled = _pool_group(ext[:, cols], u[:, cols], window, tile_idx)
        y = jnp.dot(pooled.astype(BF16), poolw_ref[g], preferred_element_type=F32)
        mixed.append(y * pscale_ref[:, cols])

    for hd in range(HGRN_HEADS):
        cols = slice(hd * LANES, (hd + 1) * LANES)

        def part(k):
            return h[:, d_pool + k * d_hgrn + hd * LANES: d_pool + k * d_hgrn + (hd + 1) * LANES]

        o, new_state = _hgrn_head(part(0), part(1), part(2), lb_ref[:, cols], state_ref[hd], masks_ref)
        state_ref[hd] = new_state
        o = o * lax.rsqrt(jnp.mean(o * o, axis=-1, keepdims=True) + RMS_EPS) * ng_ref[...]
        mixed.append(o * _silu(part(3)))

    mix_in = jnp.concatenate(mixed, axis=1).astype(BF16)
    mix = jnp.dot(mix_in, wout_ref[...], preferred_element_type=F32)
    o_ref[...] = _layer_norm(alpha * x + mix, lng_ref[...], lnb_ref[...])


def _mixer(x3d, w_in, pool_w, pool_scale, lb, norm_g, w_out, ln_g, ln_b, masks, *, alpha):
    bsz, seq, d = x3d.shape
    d_in = w_in.shape[1]
    d_mix = w_out.shape[0]
    assert seq % TILE == 0
    n_lvl = masks.shape[0]
    vmem = (2 * 2 * TILE * d * 4
            + 2 * 2 * (d * d_in + d_mix * d + pool_w.size)
            + 2 * n_lvl * TILE * TILE * 4
            + 6 * TILE * d_in * 4
            + 8 * TILE * TILE * 4)
    full = lambda *shape: pl.BlockSpec(shape, lambda b, s: (0,) * len(shape))
    return pl.pallas_call(
        functools.partial(_mixer_kernel, alpha=alpha),
        grid=(bsz, seq // TILE),
        in_specs=[
            pl.BlockSpec((None, TILE, d), lambda b, s: (b, s, 0)),
            full(d, d_in),
            full(*pool_w.shape),
            full(1, pool_scale.shape[-1]),
            full(1, lb.shape[-1]),
            full(1, norm_g.shape[-1]),
            full(d_mix, d),
            full(1, d),
            full(1, d),
            full(*masks.shape),
        ],
        out_specs=pl.BlockSpec((None, TILE, d), lambda b, s: (b, s, 0)),
        out_shape=jax.ShapeDtypeStruct((bsz, seq, d), F32),
        scratch_shapes=[
            pltpu.VMEM((HGRN_HEADS, LANES, LANES), F32),
            pltpu.VMEM((POOL_HALO * SUBLANES, LANES * len(POOL_WINDOWS)), F32),
        ],
        compiler_params=pltpu.CompilerParams(
            dimension_semantics=("parallel", "arbitrary"),
            vmem_limit_bytes=int(vmem * 1.25)),
        name="mixer",
    )(x3d, w_in, pool_w, pool_scale.reshape(1, -1), lb.reshape(1, -1), norm_g.reshape(1, -1), w_out,
      ln_g.reshape(1, d), ln_b.reshape(1, d), masks)


def _interleave(x):
    bsz, seq, d = x.shape
    x = x.reshape(bsz, seq // TILE, SUBLANES, SEG, d)
    return jnp.swapaxes(x, 2, 3).reshape(bsz, seq, d)


def _deinterleave(x):
    bsz, seq, d = x.shape
    x = x.reshape(bsz, seq // TILE, SEG, SUBLANES, d)
    return jnp.swapaxes(x, 2, 3).reshape(bsz, seq, d)


def kernel(x, w_in, pool_w, pool_scale, lb_param, hgrn_norm_g, w_out, ffn1_gate, ffn1_up, ffn1_down,
           ffn2_gate, ffn2_up, ffn2_down, ln_g, ln_b):
    bsz, seq, d = x.shape
    depth = w_in.shape[0]
    alpha = (2.0 * depth) ** 0.25

    lb_all = jnp.cumsum(jax.nn.softmax(lb_param.astype(F32), axis=0), axis=0)
    lb_all = lb_all - lb_all[0:1]

    masks = jnp.asarray(_level_masks())
    cast = lambda w: w.astype(BF16)
    w_in_b, pool_w_b, w_out_b = cast(w_in), cast(pool_w), cast(w_out)
    f1g, f1u, f1d = cast(ffn1_gate), cast(ffn1_up), cast(ffn1_down)
    f2g, f2u, f2d = cast(ffn2_gate), cast(ffn2_up), cast(ffn2_down)

    xp = _interleave(x)
    for l in range(depth):
        x2 = _ffn(xp.reshape(bsz * seq, d), f1g[l], f1u[l], f1d[l], ln_g[l, 0], ln_b[l, 0], alpha=alpha)
        xm = _mixer(x2.reshape(bsz, seq, d), w_in_b[l], pool_w_b[l], pool_scale[l], lb_all[l],
                    hgrn_norm_g[l], w_out_b[l], ln_g[l, 1], ln_b[l, 1], masks, alpha=alpha)
        x2 = _ffn(xm.reshape(bsz * seq, d), f2g[l], f2u[l], f2d[l], ln_g[l, 2], ln_b[l, 2], alpha=alpha)
        xp = x2.reshape(bsz, seq, d)
    return _deinterleave(xp)
```

```python
import functools

import numpy as np
import jax
import jax.numpy as jnp
from jax import lax
from jax.experimental import pallas as pl
from jax.experimental.pallas import tpu as pltpu

F32 = jnp.float32
BF16 = jnp.bfloat16

SUBLANES = 8
LANES = 128
MXU_DIM = 256

POOL_WINDOWS = (2, 4, 8, 16)
HGRN_HEADS = 4
LN_EPS = 1e-5
RMS_EPS = 1e-6
LOG2E = 1.4426950408889634

TILE = 256
SEG = TILE // SUBLANES
FFN_ROWS = 512
FFN_COLS = MXU_DIM
FFN_LN_SLICES = 8
POOL_HALO = max(POOL_WINDOWS) - 1


def _layer_norm(y, g, b):
    mu = jnp.mean(y, axis=-1, keepdims=True)
    yc = y - mu
    var = jnp.mean(yc * yc, axis=-1, keepdims=True)
    return yc * lax.rsqrt(var + LN_EPS) * g + b


def _silu(x):
    return x * jax.nn.sigmoid(x)


def _resident(shape, index_map):
    return pl.BlockSpec(shape, index_map, pipeline_mode=pl.Buffered(1))


def _ffn_kernel(x_ref, wg_ref, wu_ref, wd_ref, g_ref, b_ref, o_ref, xb_ref, hid_ref, acc_ref, *, alpha,
                n_tiles):
    i = pl.program_id(0)
    dff = wg_ref.shape[1]
    cur = i % 2
    prev = 1 - cur
    ln_rows = FFN_ROWS // FFN_LN_SLICES

    def finish_slice(k):
        rows = slice(k * ln_rows, (k + 1) * ln_rows)
        o_ref[rows, :] = _layer_norm(0.5 * acc_ref[prev, rows, :], g_ref[...], b_ref[...])

    @pl.when(i == 0)
    def _():
        acc_ref[1] = jnp.zeros(acc_ref.shape[1:], F32)

    @pl.when(i < n_tiles)
    def _():
        x = x_ref[...]
        xb_ref[...] = x.astype(BF16)
        for c in range(dff // FFN_COLS):
            cols = slice(c * FFN_COLS, (c + 1) * FFN_COLS)
            xb = xb_ref[...]
            gate = jnp.dot(xb, wg_ref[:, cols], preferred_element_type=F32)
            up = jnp.dot(xb, wu_ref[:, cols], preferred_element_type=F32)
            hid_ref[:, cols] = (_silu(gate) * up).astype(BF16)
            if c < FFN_LN_SLICES:
                finish_slice(c)
        acc_ref[cur] = (2.0 * alpha) * x + jnp.dot(hid_ref[...], wd_ref[...], preferred_element_type=F32)

    @pl.when(i == n_tiles)
    def _():
        for k in range(FFN_LN_SLICES):
            finish_slice(k)


def _ffn(x2d, wg, wu, wd, g, b, layer, *, alpha):
    rows, d = x2d.shape
    dff = wg.shape[2]
    assert rows % FFN_ROWS == 0 and dff % FFN_COLS == 0
    assert FFN_LN_SLICES <= dff // FFN_COLS and FFN_ROWS % FFN_LN_SLICES == 0
    n_tiles = rows // FFN_ROWS
    vmem = (3 * d * dff * 2
            + 2 * 2 * FFN_ROWS * d * 4
            + FFN_ROWS * d * (2 + 2 * 4)
            + FFN_ROWS * dff * 2
            + 3 * FFN_ROWS * FFN_COLS * 4
            + FFN_ROWS * d * 4)
    return pl.pallas_call(
        functools.partial(_ffn_kernel, alpha=alpha, n_tiles=n_tiles),
        grid=(n_tiles + 1,),
        in_specs=[
            pl.BlockSpec((FFN_ROWS, d), lambda i: (jnp.minimum(i, n_tiles - 1), 0)),
            _resident((None, d, dff), lambda i: (layer, 0, 0)),
            _resident((None, d, dff), lambda i: (layer, 0, 0)),
            _resident((None, dff, d), lambda i: (layer, 0, 0)),
            _resident((1, d), lambda i: (0, 0)),
            _resident((1, d), lambda i: (0, 0)),
        ],
        out_specs=pl.BlockSpec((FFN_ROWS, d), lambda i: (jnp.maximum(i - 1, 0), 0)),
        out_shape=jax.ShapeDtypeStruct((rows, d), F32),
        scratch_shapes=[pltpu.VMEM((FFN_ROWS, d), BF16), pltpu.VMEM((FFN_ROWS, dff), BF16),
                        pltpu.VMEM((2, FFN_ROWS, d), F32)],
        compiler_params=pltpu.CompilerParams(
            dimension_semantics=("arbitrary",),
            vmem_limit_bytes=int(vmem * 1.1)),
        name="ffn",
    )(x2d, wg, wu, wd, g.reshape(1, d), b.reshape(1, d))


def _level_plan():
    plan = []
    m = SUBLANES
    while m >= 2:
        plan.append(("seg", m))
        m //= 2
    m = SEG
    while m >= 2:
        plan.append(("pos", m))
        m //= 2
    return plan


def _level_masks():
    r = np.arange(TILE)
    seg, pos = r % SUBLANES, r // SUBLANES
    masks = []
    for kind, m in _level_plan():
        h = m // 2
        if kind == "seg":
            same = (seg[:, None] // m) == (seg[None, :] // m)
            late, early = (seg % m) >= h, (seg % m) < h
        else:
            same = (seg[:, None] == seg[None, :]) & ((pos[:, None] // m) == (pos[None, :] // m))
            late, early = (pos % m) >= h, (pos % m) < h
        masks.append(same & late[:, None] & early[None, :])
    masks = np.stack(masks)
    time = seg * SEG + pos
    assert (masks.sum(0) == (time[None, :] < time[:, None])).all()
    return masks.astype(np.float32)


def _pool_group(ext, u_g, window, tile_in_seq):
    cur = ext[(POOL_HALO - (window - 1)) * SUBLANES:]
    span = 1
    while span < window:
        cur = cur[span * SUBLANES:] + cur[:-span * SUBLANES]
        span *= 2
    row = lax.broadcasted_iota(jnp.int32, (TILE, LANES), 0)
    pos = tile_in_seq * TILE + (row & (SUBLANES - 1)) * SEG + (row >> 3)
    cnt = jnp.minimum(pos + 1, window).astype(F32)
    return cur / cnt - u_g


def _gates(f_raw, lb):
    t = jnp.exp(-jnp.abs(f_raw))
    log_sig = jnp.minimum(f_raw, 0.0) - jnp.log(1.0 + t)
    a = jnp.log(lb)
    c = jnp.log1p(-lb) + log_sig
    log_f = jnp.maximum(a, c) + jnp.log(1.0 + jnp.exp(-jnp.abs(a - c)))
    inv = 1.0 / (1.0 + t)
    sig_neg = jnp.where(f_raw >= 0.0, t * inv, inv)
    return log_f * LOG2E, (1.0 - lb) * sig_neg


def _vregs(a):
    return [a[j * SUBLANES:(j + 1) * SUBLANES] for j in range(a.shape[0] // SUBLANES)]


def _hgrn_head(q, f_raw, v, lb, state_t, masks_ref):
    lf, kk = _gates(f_raw, lb)
    qs = q * (LANES ** -0.5)

    lf_v = _vregs(lf)
    b_v = [lf_v[0]]
    for j in range(1, SEG):
        b_v.append(b_v[-1] + lf_v[j])
    b_loc = jnp.concatenate(b_v, axis=0)
    seg_tot = b_v[-1]

    sub = lax.broadcasted_iota(jnp.int32, (SUBLANES, LANES), 0)
    inc = seg_tot
    for sh in (1, 2, 4):
        inc = inc + jnp.where(sub >= sh, pltpu.roll(inc, sh, 0), 0.0)
    off = inc - seg_tot
    total = inc[SUBLANES - 1:SUBLANES, :]
    bg = b_loc + jnp.concatenate([off] * SEG, axis=0)

    q_in = (qs * jnp.exp2(bg)).astype(BF16)
    o = lax.dot_general(q_in, state_t.astype(BF16), (((1,), (1,)), ((), ())),
                        preferred_element_type=F32)
    k_dec = (kk * jnp.exp2(total - bg)).astype(BF16)
    v_b = v.astype(BF16)
    new_state_t = state_t * jnp.exp2(total) + lax.dot_general(
        v_b, k_dec, (((0,), (0,)), ((), ())), preferred_element_type=F32)

    row_sub = lax.broadcasted_iota(jnp.int32, (TILE, LANES), 0) & (SUBLANES - 1)
    scores = None
    for lvl, (kind, m) in enumerate(_level_plan()):
        h = m // 2
        if kind == "seg":
            ref = off
            for blk in range(SUBLANES // m):
                src = blk * m + h
                in_blk = (sub >= blk * m) & (sub < (blk + 1) * m)
                ref = jnp.where(in_blk, jnp.broadcast_to(off[src:src + 1, :], off.shape), ref)
            t = b_loc + jnp.concatenate([off - ref] * SEG, axis=0)
            z = jnp.where((row_sub & (m - 1)) >= h, t, -t)
        else:
            z_v = []
            for j in range(SEG):
                ref_v = b_v[(j // m) * m + h - 1]
                z_v.append(b_v[j] - ref_v if j % m >= h else ref_v - b_v[j])
            z = jnp.concatenate(z_v, axis=0)
        x = jnp.exp2(z)
        part = lax.dot_general((x * qs).astype(BF16), (x * kk).astype(BF16),
                               (((1,), (1,)), ((), ())), preferred_element_type=F32)
        part = masks_ref[lvl] * part
        scores = part if scores is None else scores + part

    o = o + jnp.dot(scores.astype(BF16), v_b, preferred_element_type=F32)
    o = o + jnp.sum(qs * kk, axis=-1, keepdims=True) * v
    return o, new_state_t


def _mixer_kernel(xa_ref, xc_ref, win_ref, poolw_ref, pscale_ref, lb_ref, ng_ref, wout_ref, lng_ref,
                  lnb_ref, masks_ref, o_ref, xab_ref, h_ref, mix_ref, state_ref, tail_ref, *, alpha,
                  tiles_per_seq):
    g = pl.program_id(0)
    d_pool = LANES * len(POOL_WINDOWS)
    d_hgrn = LANES * HGRN_HEADS
    halo_rows = POOL_HALO * SUBLANES

    @pl.when(g == 0)
    def _():
        h_ref[...] = jnp.zeros_like(h_ref)
        mix_ref[...] = jnp.zeros_like(mix_ref)
        state_ref[...] = jnp.zeros_like(state_ref)
        tail_ref[...] = jnp.zeros_like(tail_ref)

    d_out = o_ref.shape[1]
    tile_in_seq = (g + tiles_per_seq - 1) % tiles_per_seq
    first = tile_in_seq == 0

    for parity in range(2):
        pl.when(g % 2 == parity)(functools.partial(
            _mixer_step, parity, 1 - parity, first, tile_in_seq, xa_ref, xc_ref, win_ref, poolw_ref,
            pscale_ref, lb_ref, ng_ref, wout_ref, lng_ref, lnb_ref, masks_ref, o_ref, xab_ref, h_ref,
            mix_ref, state_ref, tail_ref, alpha=alpha))


def _mixer_step(slot_a, slot_b, first, tile_in_seq, xa_ref, xc_ref, win_ref, poolw_ref, pscale_ref, lb_ref,
                ng_ref, wout_ref, lng_ref, lnb_ref, masks_ref, o_ref, xab_ref, h_ref, mix_ref, state_ref,
                tail_ref, *, alpha):
    d_pool = LANES * len(POOL_WINDOWS)
    d_hgrn = LANES * HGRN_HEADS
    halo_rows = POOL_HALO * SUBLANES
    d_out = o_ref.shape[1]
    xab_ref[...] = xa_ref[...].astype(BF16)

    def project(part):
        cols = slice(part * d_hgrn, (part + 1) * d_hgrn)
        h_ref[slot_a, :, cols] = jnp.dot(xab_ref[...], win_ref[:, cols], preferred_element_type=F32)

    def finish(part):
        cols = slice(part * d_out // HGRN_HEADS, (part + 1) * d_out // HGRN_HEADS)
        o_ref[:, cols] = alpha * xc_ref[:, cols] + jnp.dot(mix_ref[slot_a], wout_ref[:, cols],
                                                          preferred_element_type=F32)

    project(0)
    hb = h_ref.at[slot_b]

    u = hb[:, :d_pool]
    cur_tail = u[TILE - halo_rows:, :]
    sub = lax.broadcasted_iota(jnp.int32, cur_tail.shape, 0) & (SUBLANES - 1)
    prev_tail = jnp.where(first, 0.0, tail_ref[...])
    head = jnp.where(sub == 0,
                     pltpu.roll(prev_tail, halo_rows - (SUBLANES - 1), 0),
                     pltpu.roll(cur_tail, 1, 0))
    tail_ref[...] = cur_tail
    ext = jnp.concatenate([head, u], axis=0)
    for grp, window in enumerate(POOL_WINDOWS):
        cols = slice(grp * LANES, (grp + 1) * LANES)
        pooled = _pool_group(ext[:, cols], u[:, cols], window, tile_in_seq)
        y = jnp.dot(pooled.astype(BF16), poolw_ref[grp], preferred_element_type=F32)
        mix_ref[slot_b, :, cols] = (y * pscale_ref[:, cols]).astype(BF16)

    for hd in range(HGRN_HEADS):
        cols = slice(hd * LANES, (hd + 1) * LANES)
        project(hd + 1)

        def part(k):
            lo = d_pool + k * d_hgrn + hd * LANES
            return hb[:, lo:lo + LANES]

        state = jnp.where(first, 0.0, state_ref[hd])
        o, new_state = _hgrn_head(part(0), part(1), part(2), lb_ref[:, cols], state, masks_ref)
        state_ref[hd] = new_state
        o = o * lax.rsqrt(jnp.mean(o * o, axis=-1, keepdims=True) + RMS_EPS) * ng_ref[...]
        mix_ref[slot_b, :, d_pool + hd * LANES:d_pool + (hd + 1) * LANES] = (o * _silu(part(3))).astype(BF16)
        finish(hd)

    o_ref[...] = _layer_norm(o_ref[...], lng_ref[...], lnb_ref[...])


def _mixer(x3d, w_in, pool_w, pool_scale, lb, norm_g, w_out, ln_g, ln_b, masks, layer, *, alpha):
    bsz, seq, d = x3d.shape
    d_in = w_in.shape[2]
    d_mix = w_out.shape[1]
    assert seq % TILE == 0
    tiles_per_seq = seq // TILE
    n_tiles = bsz * tiles_per_seq
    n_lvl = masks.shape[0]
    x_tiles = x3d.reshape(n_tiles, TILE, d)
    last = n_tiles - 1
    vmem = (3 * 2 * TILE * d * 4
            + 2 * (d * d_in + d_mix * d + pool_w[0].size)
            + n_lvl * TILE * TILE * 4
            + 2 * TILE * d_in * 4 + 2 * TILE * d_mix * 2
            + 4 * TILE * d_in * 4
            + 8 * TILE * TILE * 4)
    const = lambda *shape: _resident(shape, lambda g: (0,) * len(shape))
    at_layer = lambda *shape: _resident((None,) + shape, lambda g: (layer,) + (0,) * len(shape))
    out = pl.pallas_call(
        functools.partial(_mixer_kernel, alpha=alpha, tiles_per_seq=tiles_per_seq),
        grid=(n_tiles + 2,),
        in_specs=[
            pl.BlockSpec((None, TILE, d), lambda g: (jnp.minimum(g, last), 0, 0)),
            pl.BlockSpec((None, TILE, d), lambda g: (jnp.clip(g - 2, 0, last), 0, 0)),
            at_layer(d, d_in),
            at_layer(*pool_w.shape[1:]),
            const(1, pool_scale.shape[-1]),
            const(1, lb.shape[-1]),
            const(1, norm_g.shape[-1]),
            at_layer(d_mix, d),
            const(1, d),
            const(1, d),
            const(*masks.shape),
        ],
        out_specs=pl.BlockSpec((None, TILE, d), lambda g: (jnp.clip(g - 2, 0, last), 0, 0)),
        out_shape=jax.ShapeDtypeStruct((n_tiles, TILE, d), F32),
        scratch_shapes=[
            pltpu.VMEM((TILE, d), BF16),
            pltpu.VMEM((2, TILE, d_in), F32),
            pltpu.VMEM((2, TILE, d_mix), BF16),
            pltpu.VMEM((HGRN_HEADS, LANES, LANES), F32),
            pltpu.VMEM((POOL_HALO * SUBLANES, LANES * len(POOL_WINDOWS)), F32),
        ],
        compiler_params=pltpu.CompilerParams(
            dimension_semantics=("arbitrary",),
            vmem_limit_bytes=int(vmem * 1.2)),
        name="mixer",
    )(x_tiles, x_tiles, w_in, pool_w, pool_scale.reshape(1, -1), lb.reshape(1, -1), norm_g.reshape(1, -1),
      w_out, ln_g.reshape(1, d), ln_b.reshape(1, d), masks)
    return out.reshape(bsz, seq, d)


def _interleave(x):
    bsz, seq, d = x.shape
    x = x.reshape(bsz, seq // TILE, SUBLANES, SEG, d)
    return jnp.swapaxes(x, 2, 3).reshape(bsz, seq, d)


def _deinterleave(x):
    bsz, seq, d = x.shape
    x = x.reshape(bsz, seq // TILE, SEG, SUBLANES, d)
    return jnp.swapaxes(x, 2, 3).reshape(bsz, seq, d)


def kernel(x, w_in, pool_w, pool_scale, lb_param, hgrn_norm_g, w_out, ffn1_gate, ffn1_up, ffn1_down,
           ffn2_gate, ffn2_up, ffn2_down, ln_g, ln_b):
    bsz, seq, d = x.shape
    depth = w_in.shape[0]
    alpha = (2.0 * depth) ** 0.25

    lb_all = jnp.cumsum(jax.nn.softmax(lb_param.astype(F32), axis=0), axis=0)
    lb_all = lb_all - lb_all[0:1]

    masks = jnp.asarray(_level_masks())
    cast = lambda w: w.astype(BF16)
    w_in_b, pool_w_b, w_out_b = cast(w_in), cast(pool_w), cast(w_out)
    f1g, f1u, f1d = cast(ffn1_gate), cast(ffn1_up), cast(ffn1_down)
    f2g, f2u, f2d = cast(ffn2_gate), cast(ffn2_up), cast(ffn2_down)

    xp = _interleave(x)
    for l in range(depth):
        x2 = _ffn(xp.reshape(bsz * seq, d), f1g, f1u, f1d, ln_g[l, 0], ln_b[l, 0], l, alpha=alpha)
        xm = _mixer(x2.reshape(bsz, seq, d), w_in_b, pool_w_b, pool_scale[l], lb_all[l],
                    hgrn_norm_g[l], w_out_b, ln_g[l, 1], ln_b[l, 1], masks, l, alpha=alpha)
        x2 = _ffn(xm.reshape(bsz * seq, d), f2g, f2u, f2d, ln_g[l, 2], ln_b[l, 2], l, alpha=alpha)
        xp = x2.reshape(bsz, seq, d)
    return _deinterleave(xp)
```

```python
import functools

import numpy as np
import jax
import jax.numpy as jnp
from jax import lax
from jax.experimental import pallas as pl
from jax.experimental.pallas import tpu as pltpu

F32 = jnp.float32
BF16 = jnp.bfloat16

SUBLANES = 8
LANES = 128
MXU_DIM = 256

POOL_WINDOWS = (2, 4, 8, 16)
HGRN_HEADS = 4
LN_EPS = 1e-5
RMS_EPS = 1e-6
LOG2E = 1.4426950408889634

TILE = 256
SEG = TILE // SUBLANES
FFN_ROWS = 512
FFN_COLS = MXU_DIM
FFN_LN_SLICES = 8
POOL_HALO = max(POOL_WINDOWS) - 1
MERGE_DEPTHS = 2
MERGE_LOG2_SPAN = 120.0


def _layer_norm(y, g, b):
    mu = jnp.mean(y, axis=-1, keepdims=True)
    yc = y - mu
    var = jnp.mean(yc * yc, axis=-1, keepdims=True)
    return yc * lax.rsqrt(var + LN_EPS) * g + b


def _silu(x):
    return x * jax.nn.sigmoid(x)


def _resident(shape, index_map):
    return pl.BlockSpec(shape, index_map, pipeline_mode=pl.Buffered(1))


def _ffn_kernel(x_ref, wg_ref, wu_ref, wd_ref, g_ref, b_ref, o_ref, xb_ref, hid_ref, acc_ref, *, alpha,
                n_tiles):
    i = pl.program_id(0)
    dff = wg_ref.shape[1]
    cur = i % 2
    prev = 1 - cur
    ln_rows = FFN_ROWS // FFN_LN_SLICES

    def finish_slice(k):
        rows = slice(k * ln_rows, (k + 1) * ln_rows)
        o_ref[rows, :] = _layer_norm(0.5 * acc_ref[prev, rows, :], g_ref[...], b_ref[...])

    @pl.when(i == 0)
    def _():
        acc_ref[1] = jnp.zeros(acc_ref.shape[1:], F32)

    @pl.when(i < n_tiles)
    def _():
        x = x_ref[...]
        xb_ref[...] = x.astype(BF16)
        for c in range(dff // FFN_COLS):
            cols = slice(c * FFN_COLS, (c + 1) * FFN_COLS)
            xb = xb_ref[...]
            gate = jnp.dot(xb, wg_ref[:, cols], preferred_element_type=F32)
            up = jnp.dot(xb, wu_ref[:, cols], preferred_element_type=F32)
            hid_ref[:, cols] = (_silu(gate) * up).astype(BF16)
            if c < FFN_LN_SLICES:
                finish_slice(c)
        acc_ref[cur] = (2.0 * alpha) * x + jnp.dot(hid_ref[...], wd_ref[...], preferred_element_type=F32)

    @pl.when(i == n_tiles)
    def _():
        for k in range(FFN_LN_SLICES):
            finish_slice(k)


def _ffn(x2d, wg, wu, wd, g, b, layer, *, alpha):
    rows, d = x2d.shape
    dff = wg.shape[2]
    assert rows % FFN_ROWS == 0 and dff % FFN_COLS == 0
    assert FFN_LN_SLICES <= dff // FFN_COLS and FFN_ROWS % FFN_LN_SLICES == 0
    n_tiles = rows // FFN_ROWS
    vmem = (3 * d * dff * 2
            + 2 * 2 * FFN_ROWS * d * 4
            + FFN_ROWS * d * (2 + 2 * 4)
            + FFN_ROWS * dff * 2
            + 3 * FFN_ROWS * FFN_COLS * 4
            + FFN_ROWS * d * 4)
    return pl.pallas_call(
        functools.partial(_ffn_kernel, alpha=alpha, n_tiles=n_tiles),
        grid=(n_tiles + 1,),
        in_specs=[
            pl.BlockSpec((FFN_ROWS, d), lambda i: (jnp.minimum(i, n_tiles - 1), 0)),
            _resident((None, d, dff), lambda i: (layer, 0, 0)),
            _resident((None, d, dff), lambda i: (layer, 0, 0)),
            _resident((None, dff, d), lambda i: (layer, 0, 0)),
            _resident((1, d), lambda i: (0, 0)),
            _resident((1, d), lambda i: (0, 0)),
        ],
        out_specs=pl.BlockSpec((FFN_ROWS, d), lambda i: (jnp.maximum(i - 1, 0), 0)),
        out_shape=jax.ShapeDtypeStruct((rows, d), F32),
        scratch_shapes=[pltpu.VMEM((FFN_ROWS, d), BF16), pltpu.VMEM((FFN_ROWS, dff), BF16),
                        pltpu.VMEM((2, FFN_ROWS, d), F32)],
        compiler_params=pltpu.CompilerParams(
            dimension_semantics=("arbitrary",),
            vmem_limit_bytes=int(vmem * 1.1)),
        name="ffn",
    )(x2d, wg, wu, wd, g.reshape(1, d), b.reshape(1, d))


def _level_plan():
    plan = []
    m = SUBLANES
    while m >= 2:
        plan.append(("seg", m))
        m //= 2
    m = SEG
    while m >= 2:
        plan.append(("pos", m))
        m //= 2
    return plan


def _level_masks():
    r = np.arange(TILE)
    seg, pos = r % SUBLANES, r // SUBLANES
    masks = []
    for kind, m in _level_plan():
        h = m // 2
        if kind == "seg":
            same = (seg[:, None] // m) == (seg[None, :] // m)
            late, early = (seg % m) >= h, (seg % m) < h
        else:
            same = (seg[:, None] == seg[None, :]) & ((pos[:, None] // m) == (pos[None, :] // m))
            late, early = (pos % m) >= h, (pos % m) < h
        masks.append(same & late[:, None] & early[None, :])
    masks = np.stack(masks)
    time = seg * SEG + pos
    causal = time[None, :] < time[:, None]
    assert (masks.sum(0) == causal).all()
    merged = [causal & ((seg[:, None] >> (3 - d)) == (seg[None, :] >> (3 - d))) for d in range(MERGE_DEPTHS)]
    for d in range(MERGE_DEPTHS):
        assert (masks[:d].sum(0) + merged[d] == causal).all()
    return np.concatenate([masks, np.stack(merged)]).astype(np.float32)


def _pool_group(ext, u_g, window, tile_in_seq):
    cur = ext[(POOL_HALO - (window - 1)) * SUBLANES:]
    span = 1
    while span < window:
        cur = cur[span * SUBLANES:] + cur[:-span * SUBLANES]
        span *= 2
    row = lax.broadcasted_iota(jnp.int32, (TILE, LANES), 0)
    pos = tile_in_seq * TILE + (row & (SUBLANES - 1)) * SEG + (row >> 3)
    cnt = jnp.minimum(pos + 1, window).astype(F32)
    return cur / cnt - u_g


def _gates(f_raw, lb):
    t = jnp.exp(-jnp.abs(f_raw))
    log_sig = jnp.minimum(f_raw, 0.0) - jnp.log(1.0 + t)
    a = jnp.log(lb)
    c = jnp.log1p(-lb) + log_sig
    log_f = jnp.maximum(a, c) + jnp.log(1.0 + jnp.exp(-jnp.abs(a - c)))
    inv = 1.0 / (1.0 + t)
    sig_neg = jnp.where(f_raw >= 0.0, t * inv, inv)
    return log_f * LOG2E, (1.0 - lb) * sig_neg


def _vregs(a):
    return [a[j * SUBLANES:(j + 1) * SUBLANES] for j in range(a.shape[0] // SUBLANES)]


class _Head:
    def __init__(self, **kw):
        self.__dict__.update(kw)


def _seg_split_offset(off, sub, m):
    ref = off
    for blk in range(SUBLANES // m):
        src = blk * m + m // 2
        in_blk = (sub >= blk * m) & (sub < (blk + 1) * m)
        ref = jnp.where(in_blk, jnp.broadcast_to(off[src:src + 1, :], off.shape), ref)
    return off - ref


def _hgrn_prep(q, f_raw, v, lb, state_t):
    lf, kk = _gates(f_raw, lb)
    qs = q * (LANES ** -0.5)

    lf_v = _vregs(lf)
    b_v = [lf_v[0]]
    for j in range(1, SEG):
        b_v.append(b_v[-1] + lf_v[j])
    b_loc = jnp.concatenate(b_v, axis=0)
    seg_tot = b_v[-1]

    sub = lax.broadcasted_iota(jnp.int32, (SUBLANES, LANES), 0)
    inc = seg_tot
    for sh in (1, 2, 4):
        inc = inc + jnp.where(sub >= sh, pltpu.roll(inc, sh, 0), 0.0)
    off = inc - seg_tot
    total = inc[SUBLANES - 1:SUBLANES, :]
    bg = b_loc + jnp.concatenate([off] * SEG, axis=0)

    q_in = (qs * jnp.exp2(bg)).astype(BF16)
    o = lax.dot_general(q_in, state_t.astype(BF16), (((1,), (1,)), ((), ())),
                        preferred_element_type=F32)
    k_dec = (kk * jnp.exp2(total - bg)).astype(BF16)
    v_b = v.astype(BF16)
    new_state_t = state_t * jnp.exp2(total) + lax.dot_general(
        v_b, k_dec, (((0,), (0,)), ((), ())), preferred_element_type=F32)

    bounds = [off[(SUBLANES >> (d + 1)) * k:(SUBLANES >> (d + 1)) * k + 1, :] if k else jnp.zeros_like(total)
              for d in range(MERGE_DEPTHS) for k in range(2 << d)]
    span = []
    pos = 0
    for d in range(MERGE_DEPTHS):
        edges = bounds[pos:pos + (2 << d)] + [total]
        pos += 2 << d
        worst = edges[0] - edges[1]
        for k in range(1, 2 << d):
            worst = jnp.maximum(worst, edges[k] - edges[k + 1])
        span.append(worst)
    return _Head(qs=qs, kk=kk, v=v, v_b=v_b, b_v=b_v, b_loc=b_loc, off=off, sub=sub, o=o, span=span), new_state_t


def _pair_scores(head, x_q, x_k):
    return lax.dot_general((x_q * head.qs).astype(BF16), (x_k * head.kk).astype(BF16),
                           (((1,), (1,)), ((), ())), preferred_element_type=F32)


def _level_scores(head, lvl, masks_ref):
    kind, m = _level_plan()[lvl]
    h = m // 2
    if kind == "seg":
        row_sub = lax.broadcasted_iota(jnp.int32, (TILE, LANES), 0) & (SUBLANES - 1)
        t = head.b_loc + jnp.concatenate([_seg_split_offset(head.off, head.sub, m)] * SEG, axis=0)
        z = jnp.where((row_sub & (m - 1)) >= h, t, -t)
    else:
        b_v = head.b_v
        z_v = []
        for j in range(SEG):
            ref_v = b_v[(j // m) * m + h - 1]
            z_v.append(b_v[j] - ref_v if j % m >= h else ref_v - b_v[j])
        z = jnp.concatenate(z_v, axis=0)
    x = jnp.exp2(z)
    return masks_ref[lvl] * _pair_scores(head, x, x)


def _merged_scores(head, depth, masks_ref):
    t = head.b_loc + jnp.concatenate(
        [_seg_split_offset(head.off, head.sub, SUBLANES >> depth)] * SEG, axis=0)
    part = _pair_scores(head, jnp.exp2(t), jnp.exp2(-t))
    return jnp.where(masks_ref[len(_level_plan()) + depth] != 0.0, part, 0.0)


def _hgrn_scores(head, variant, masks_ref):
    n_levels = len(_level_plan())
    parts = [_level_scores(head, lvl, masks_ref) for lvl in range(min(variant, n_levels))]
    if variant < MERGE_DEPTHS:
        parts.append(_merged_scores(head, variant, masks_ref))
    scores = parts[0]
    for p in parts[1:]:
        scores = scores + p
    return scores


def _hgrn_finish(head, scores):
    o = head.o + jnp.dot(scores.astype(BF16), head.v_b, preferred_element_type=F32)
    return o + jnp.sum(head.qs * head.kk, axis=-1, keepdims=True) * head.v


def _mixer_kernel(xa_ref, xc_ref, win_ref, poolw_ref, pscale_ref, lb_ref, ng_ref, wout_ref, lng_ref,
                  lnb_ref, masks_ref, o_ref, xab_ref, h_ref, mix_ref, pre_ref, state_ref, tail_ref, *, alpha,
                  tiles_per_seq):
    g = pl.program_id(0)

    @pl.when(g == 0)
    def _():
        h_ref[...] = jnp.zeros_like(h_ref)
        mix_ref[...] = jnp.zeros_like(mix_ref)
        pre_ref[...] = jnp.zeros_like(pre_ref)
        state_ref[...] = jnp.zeros_like(state_ref)
        tail_ref[...] = jnp.zeros_like(tail_ref)

    tile_in_seq = (g + tiles_per_seq - 1) % tiles_per_seq
    first = tile_in_seq == 0

    for parity in range(2):
        pl.when(g % 2 == parity)(functools.partial(
            _mixer_step, parity, 1 - parity, first, tile_in_seq, xa_ref, xc_ref, win_ref, poolw_ref,
            pscale_ref, lb_ref, ng_ref, wout_ref, lng_ref, lnb_ref, masks_ref, o_ref, xab_ref, h_ref,
            mix_ref, pre_ref, state_ref, tail_ref, alpha=alpha))


def _mixer_step(slot_a, slot_b, first, tile_in_seq, xa_ref, xc_ref, win_ref, poolw_ref, pscale_ref, lb_ref,
                ng_ref, wout_ref, lng_ref, lnb_ref, masks_ref, o_ref, xab_ref, h_ref, mix_ref, pre_ref,
                state_ref, tail_ref, *, alpha):
    d_pool = LANES * len(POOL_WINDOWS)
    d_hgrn = LANES * HGRN_HEADS
    halo_rows = POOL_HALO * SUBLANES
    d_out = o_ref.shape[1]
    o_ref[...] = _layer_norm(pre_ref[...], lng_ref[...], lnb_ref[...])
    xab_ref[...] = xa_ref[...].astype(BF16)

    def project(part):
        cols = slice(part * d_hgrn, (part + 1) * d_hgrn)
        h_ref[slot_a, :, cols] = jnp.dot(xab_ref[...], win_ref[:, cols], preferred_element_type=F32)

    def finish(part):
        cols = slice(part * d_out // HGRN_HEADS, (part + 1) * d_out // HGRN_HEADS)
        pre_ref[:, cols] = alpha * xc_ref[:, cols] + jnp.dot(mix_ref[slot_a], wout_ref[:, cols],
                                                            preferred_element_type=F32)

    project(0)
    hb = h_ref.at[slot_b]

    u = hb[:, :d_pool]
    cur_tail = u[TILE - halo_rows:, :]
    sub = lax.broadcasted_iota(jnp.int32, cur_tail.shape, 0) & (SUBLANES - 1)
    prev_tail = jnp.where(first, 0.0, tail_ref[...])
    head = jnp.where(sub == 0,
                     pltpu.roll(prev_tail, halo_rows - (SUBLANES - 1), 0),
                     pltpu.roll(cur_tail, 1, 0))
    tail_ref[...] = cur_tail
    ext = jnp.concatenate([head, u], axis=0)
    for grp, window in enumerate(POOL_WINDOWS):
        cols = slice(grp * LANES, (grp + 1) * LANES)
        pooled = _pool_group(ext[:, cols], u[:, cols], window, tile_in_seq)
        y = jnp.dot(pooled.astype(BF16), poolw_ref[grp], preferred_element_type=F32)
        mix_ref[slot_b, :, cols] = (y * pscale_ref[:, cols]).astype(BF16)

    def part(hd, k):
        lo = d_pool + k * d_hgrn + hd * LANES
        return hb[:, lo:lo + LANES]

    heads = []
    for hd in range(HGRN_HEADS):
        if hd < 2:
            project(hd + 1)
        state = jnp.where(first, 0.0, state_ref[hd])
        head, new_state = _hgrn_prep(part(hd, 0), part(hd, 1), part(hd, 2),
                                     lb_ref[:, hd * LANES:(hd + 1) * LANES], state)
        state_ref[hd] = new_state
        heads.append(head)

    worst = [jnp.max(functools.reduce(jnp.maximum, [hd.span[d] for hd in heads])) for d in range(MERGE_DEPTHS)]

    def mix_heads(variant):
        for hd, head in enumerate(heads):
            if hd < 2:
                project(hd + 3)
            o = _hgrn_finish(head, _hgrn_scores(head, variant, masks_ref))
            o = o * lax.rsqrt(jnp.mean(o * o, axis=-1, keepdims=True) + RMS_EPS) * ng_ref[...]
            mix_ref[slot_b, :, d_pool + hd * LANES:d_pool + (hd + 1) * LANES] = (
                o * _silu(part(hd, 3))).astype(BF16)
            finish(hd)

    ok0 = worst[0] <= MERGE_LOG2_SPAN
    ok1 = worst[1] <= MERGE_LOG2_SPAN
    pl.when(ok0)(functools.partial(mix_heads, 0))
    pl.when(jnp.logical_not(ok0) & ok1)(functools.partial(mix_heads, 1))
    pl.when(jnp.logical_not(ok0 | ok1))(functools.partial(mix_heads, len(_level_plan())))


def _mixer(x3d, w_in, pool_w, pool_scale, lb, norm_g, w_out, ln_g, ln_b, masks, layer, *, alpha):
    bsz, seq, d = x3d.shape
    d_in = w_in.shape[2]
    d_mix = w_out.shape[1]
    assert seq % TILE == 0
    tiles_per_seq = seq // TILE
    n_tiles = bsz * tiles_per_seq
    n_lvl = masks.shape[0]
    x_tiles = x3d.reshape(n_tiles, TILE, d)
    last = n_tiles - 1
    vmem = (3 * 2 * TILE * d * 4
            + 2 * (d * d_in + d_mix * d + pool_w[0].size)
            + n_lvl * TILE * TILE * 4
            + 2 * TILE * d_in * 4 + 2 * TILE * d_mix * 2 + TILE * d * (2 + 4)
            + 4 * TILE * d_in * 4
            + 8 * TILE * TILE * 4)
    const = lambda *shape: _resident(shape, lambda g: (0,) * len(shape))
    at_layer = lambda *shape: _resident((None,) + shape, lambda g: (layer,) + (0,) * len(shape))
    out = pl.pallas_call(
        functools.partial(_mixer_kernel, alpha=alpha, tiles_per_seq=tiles_per_seq),
        grid=(n_tiles + 3,),
        in_specs=[
            pl.BlockSpec((None, TILE, d), lambda g: (jnp.minimum(g, last), 0, 0)),
            pl.BlockSpec((None, TILE, d), lambda g: (jnp.clip(g - 2, 0, last), 0, 0)),
            at_layer(d, d_in),
            at_layer(*pool_w.shape[1:]),
            const(1, pool_scale.shape[-1]),
            const(1, lb.shape[-1]),
            const(1, norm_g.shape[-1]),
            at_layer(d_mix, d),
            const(1, d),
            const(1, d),
            const(*masks.shape),
        ],
        out_specs=pl.BlockSpec((None, TILE, d), lambda g: (jnp.clip(g - 3, 0, last), 0, 0)),
        out_shape=jax.ShapeDtypeStruct((n_tiles, TILE, d), F32),
        scratch_shapes=[
            pltpu.VMEM((TILE, d), BF16),
            pltpu.VMEM((2, TILE, d_in), F32),
            pltpu.VMEM((2, TILE, d_mix), BF16),
            pltpu.VMEM((TILE, d), F32),
            pltpu.VMEM((HGRN_HEADS, LANES, LANES), F32),
            pltpu.VMEM((POOL_HALO * SUBLANES, LANES * len(POOL_WINDOWS)), F32),
        ],
        compiler_params=pltpu.CompilerParams(
            dimension_semantics=("arbitrary",),
            vmem_limit_bytes=int(vmem * 1.2)),
        name="mixer",
    )(x_tiles, x_tiles, w_in, pool_w, pool_scale.reshape(1, -1), lb.reshape(1, -1), norm_g.reshape(1, -1),
      w_out, ln_g.reshape(1, d), ln_b.reshape(1, d), masks)
    return out.reshape(bsz, seq, d)


def _interleave(x):
    bsz, seq, d = x.shape
    x = x.reshape(bsz, seq // TILE, SUBLANES, SEG, d)
    return jnp.swapaxes(x, 2, 3).reshape(bsz, seq, d)


def _deinterleave(x):
    bsz, seq, d = x.shape
    x = x.reshape(bsz, seq // TILE, SEG, SUBLANES, d)
    return jnp.swapaxes(x, 2, 3).reshape(bsz, seq, d)


def kernel(x, w_in, pool_w, pool_scale, lb_param, hgrn_norm_g, w_out, ffn1_gate, ffn1_up, ffn1_down,
           ffn2_gate, ffn2_up, ffn2_down, ln_g, ln_b):
    bsz, seq, d = x.shape
    depth = w_in.shape[0]
    alpha = (2.0 * depth) ** 0.25

    lb_all = jnp.cumsum(jax.nn.softmax(lb_param.astype(F32), axis=0), axis=0)
    lb_all = lb_all - lb_all[0:1]

    masks = jnp.asarray(_level_masks())
    cast = lambda w: w.astype(BF16)
    w_in_b, pool_w_b, w_out_b = cast(w_in), cast(pool_w), cast(w_out)
    f1g, f1u, f1d = cast(ffn1_gate), cast(ffn1_up), cast(ffn1_down)
    f2g, f2u, f2d = cast(ffn2_gate), cast(ffn2_up), cast(ffn2_down)

    xp = _interleave(x)
    for l in range(depth):
        x2 = _ffn(xp.reshape(bsz * seq, d), f1g, f1u, f1d, ln_g[l, 0], ln_b[l, 0], l, alpha=alpha)
        xm = _mixer(x2.reshape(bsz, seq, d), w_in_b, pool_w_b, pool_scale[l], lb_all[l],
                    hgrn_norm_g[l], w_out_b, ln_g[l, 1], ln_b[l, 1], masks, l, alpha=alpha)
        x2 = _ffn(xm.reshape(bsz * seq, d), f2g, f2u, f2d, ln_g[l, 2], ln_b[l, 2], l, alpha=alpha)
        xp = x2.reshape(bsz, seq, d)
    return _deinterleave(xp)
```

```python
import functools

import numpy as np
import jax
import jax.numpy as jnp
from jax import lax
from jax.experimental import pallas as pl
from jax.experimental.pallas import tpu as pltpu

F32 = jnp.float32
BF16 = jnp.bfloat16

SUBLANES = 8
LANES = 128
MXU_DIM = 256

POOL_WINDOWS = (2, 4, 8, 16)
HGRN_HEADS = 4
LN_EPS = 1e-5
RMS_EPS = 1e-6
LOG2E = 1.4426950408889634

TILE = 256
SEG = TILE // SUBLANES
FFN_ROWS = 512
FFN_COLS = MXU_DIM
FFN_LN_SLICES = 8
POOL_HALO = max(POOL_WINDOWS) - 1
MERGE_DEPTHS = 2
MERGE_LOG2_SPAN = 120.0


def _layer_norm(y, g, b):
    mu = jnp.mean(y, axis=-1, keepdims=True)
    yc = y - mu
    var = jnp.mean(yc * yc, axis=-1, keepdims=True)
    return yc * lax.rsqrt(var + LN_EPS) * g + b


def _silu(x):
    return x * jax.nn.sigmoid(x)


def _resident(shape, index_map):
    return pl.BlockSpec(shape, index_map, pipeline_mode=pl.Buffered(1))


def _ffn_kernel(x_ref, wg_hbm, wu_hbm, wd_hbm, g_ref, b_ref, o_ref, wg_ref, wu_ref, wd_ref, stage_g, stage_u,
                stage_d, sems, xb_ref, hid_ref, acc_ref, *, alpha, n_tiles, layer):
    i = pl.program_id(0)
    dff = wg_ref.shape[1]
    n_chunks = dff // FFN_COLS
    cur = i % 2
    prev = 1 - cur
    ln_rows = FFN_ROWS // FFN_LN_SLICES

    def finish_slice(k):
        rows = slice(k * ln_rows, (k + 1) * ln_rows)
        y = _layer_norm(0.5 * acc_ref[prev, rows, :], g_ref[...], b_ref[...])
        o_ref[rows, :] = y
        bits = lax.bitcast_convert_type(y, jnp.uint32)
        folded = None
        for r in range(0, ln_rows, SUBLANES):
            for l in range(0, y.shape[1], LANES):
                piece = bits[r:r + SUBLANES, l:l + LANES]
                folded = piece if folded is None else folded | piece
        return lax.bitcast_convert_type((folded >> 16) >> 16, F32)

    def weight_copies(c):
        slot = c % 2
        cols = pl.ds(c * FFN_COLS, FFN_COLS)
        return (pltpu.make_async_copy(wg_hbm.at[layer, :, cols], stage_g.at[slot], sems.at[slot, 0]),
                pltpu.make_async_copy(wu_hbm.at[layer, :, cols], stage_u.at[slot], sems.at[slot, 1]),
                pltpu.make_async_copy(wd_hbm.at[layer, cols, :], stage_d.at[slot], sems.at[slot, 2]))

    @pl.when(i == 0)
    def _():
        acc_ref[1] = jnp.zeros(acc_ref.shape[1:], F32)
        for cp in weight_copies(0):
            cp.start()
        for c in range(n_chunks):
            if c + 1 < n_chunks:
                for cp in weight_copies(c + 1):
                    cp.start()
            for cp in weight_copies(c):
                cp.wait()
            slot = c % 2
            cols = slice(c * FFN_COLS, (c + 1) * FFN_COLS)
            wg_ref[:, cols] = stage_g[slot].astype(BF16)
            wu_ref[:, cols] = stage_u[slot].astype(BF16)
            wd_ref[cols, :] = stage_d[slot].astype(BF16)

    @pl.when(i < n_tiles)
    def _():
        x = x_ref[...]
        xb_ref[...] = x.astype(BF16)
        anchor = None
        for c in range(n_chunks):
            cols = slice(c * FFN_COLS, (c + 1) * FFN_COLS)
            xb = xb_ref[...]
            gate = jnp.dot(xb, wg_ref[:, cols], preferred_element_type=F32)
            up = jnp.dot(xb, wu_ref[:, cols], preferred_element_type=F32)
            if anchor is not None:
                zero = jnp.concatenate([anchor] * (FFN_COLS // LANES), axis=1)
                up = jnp.concatenate([up[:SUBLANES] + zero, up[SUBLANES:]], axis=0)
            hid_ref[:, cols] = (_silu(gate) * up).astype(BF16)
            anchor = finish_slice(c) if c < FFN_LN_SLICES else None
        acc_ref[cur] = (2.0 * alpha) * x + jnp.dot(hid_ref[...], wd_ref[...], preferred_element_type=F32)

    @pl.when(i == n_tiles)
    def _():
        for k in range(FFN_LN_SLICES):
            finish_slice(k)


def _ffn(x2d, wg, wu, wd, g, b, layer, *, alpha):
    rows, d = x2d.shape
    dff = wg.shape[2]
    assert rows % FFN_ROWS == 0 and dff % FFN_COLS == 0
    assert FFN_LN_SLICES <= dff // FFN_COLS and FFN_ROWS % FFN_LN_SLICES == 0
    n_tiles = rows // FFN_ROWS
    vmem = (3 * d * dff * 2
            + 2 * 3 * d * FFN_COLS * 4
            + 2 * 2 * FFN_ROWS * d * 4
            + FFN_ROWS * d * (2 + 2 * 4)
            + FFN_ROWS * dff * 2
            + 3 * FFN_ROWS * FFN_COLS * 4
            + FFN_ROWS * d * 4)
    in_hbm = pl.BlockSpec(memory_space=pl.ANY)
    return pl.pallas_call(
        functools.partial(_ffn_kernel, alpha=alpha, n_tiles=n_tiles, layer=layer),
        grid=(n_tiles + 1,),
        in_specs=[
            pl.BlockSpec((FFN_ROWS, d), lambda i: (jnp.minimum(i, n_tiles - 1), 0)),
            in_hbm, in_hbm, in_hbm,
            _resident((1, d), lambda i: (0, 0)),
            _resident((1, d), lambda i: (0, 0)),
        ],
        out_specs=pl.BlockSpec((FFN_ROWS, d), lambda i: (jnp.maximum(i - 1, 0), 0)),
        out_shape=jax.ShapeDtypeStruct((rows, d), F32),
        scratch_shapes=[
            pltpu.VMEM((d, dff), BF16), pltpu.VMEM((d, dff), BF16), pltpu.VMEM((dff, d), BF16),
            pltpu.VMEM((2, d, FFN_COLS), F32), pltpu.VMEM((2, d, FFN_COLS), F32),
            pltpu.VMEM((2, FFN_COLS, d), F32), pltpu.SemaphoreType.DMA((2, 3)),
            pltpu.VMEM((FFN_ROWS, d), BF16), pltpu.VMEM((FFN_ROWS, dff), BF16),
            pltpu.VMEM((2, FFN_ROWS, d), F32)],
        compiler_params=pltpu.CompilerParams(
            dimension_semantics=("arbitrary",),
            vmem_limit_bytes=int(vmem * 1.1)),
        name="ffn",
    )(x2d, wg, wu, wd, g.reshape(1, d), b.reshape(1, d))


def _level_plan():
    plan = []
    m = SUBLANES
    while m >= 2:
        plan.append(("seg", m))
        m //= 2
    m = SEG
    while m >= 2:
        plan.append(("pos", m))
        m //= 2
    return plan


def _level_masks():
    r = np.arange(TILE)
    seg, pos = r % SUBLANES, r // SUBLANES
    masks = []
    for kind, m in _level_plan():
        h = m // 2
        if kind == "seg":
            same = (seg[:, None] // m) == (seg[None, :] // m)
            late, early = (seg % m) >= h, (seg % m) < h
        else:
            same = (seg[:, None] == seg[None, :]) & ((pos[:, None] // m) == (pos[None, :] // m))
            late, early = (pos % m) >= h, (pos % m) < h
        masks.append(same & late[:, None] & early[None, :])
    masks = np.stack(masks)
    time = seg * SEG + pos
    causal = time[None, :] < time[:, None]
    assert (masks.sum(0) == causal).all()
    merged = [causal & ((seg[:, None] >> (3 - d)) == (seg[None, :] >> (3 - d))) for d in range(MERGE_DEPTHS)]
    for d in range(MERGE_DEPTHS):
        assert (masks[:d].sum(0) + merged[d] == causal).all()
    return np.concatenate([masks, np.stack(merged)]).astype(np.float32)


def _pool_group(ext, u_g, window, tile_in_seq):
    cur = ext[(POOL_HALO - (window - 1)) * SUBLANES:]
    span = 1
    while span < window:
        cur = cur[span * SUBLANES:] + cur[:-span * SUBLANES]
        span *= 2
    row = lax.broadcasted_iota(jnp.int32, (TILE, LANES), 0)
    pos = tile_in_seq * TILE + (row & (SUBLANES - 1)) * SEG + (row >> 3)
    cnt = jnp.minimum(pos + 1, window).astype(F32)
    return cur / cnt - u_g


def _gates(f_raw, lb):
    t = jnp.exp(-jnp.abs(f_raw))
    log_sig = jnp.minimum(f_raw, 0.0) - jnp.log(1.0 + t)
    a = jnp.log(lb)
    c = jnp.log1p(-lb) + log_sig
    log_f = jnp.maximum(a, c) + jnp.log(1.0 + jnp.exp(-jnp.abs(a - c)))
    inv = 1.0 / (1.0 + t)
    sig_neg = jnp.where(f_raw >= 0.0, t * inv, inv)
    return log_f * LOG2E, (1.0 - lb) * sig_neg


def _vregs(a):
    return [a[j * SUBLANES:(j + 1) * SUBLANES] for j in range(a.shape[0] // SUBLANES)]


class _Head:
    def __init__(self, **kw):
        self.__dict__.update(kw)


def _seg_split_offset(off, sub, m):
    ref = off
    for blk in range(SUBLANES // m):
        src = blk * m + m // 2
        in_blk = (sub >= blk * m) & (sub < (blk + 1) * m)
        ref = jnp.where(in_blk, jnp.broadcast_to(off[src:src + 1, :], off.shape), ref)
    return off - ref


def _hgrn_prep(q, f_raw, v, lb, state_t):
    lf, kk = _gates(f_raw, lb)
    qs = q * (LANES ** -0.5)

    lf_v = _vregs(lf)
    b_v = [lf_v[0]]
    for j in range(1, SEG):
        b_v.append(b_v[-1] + lf_v[j])
    b_loc = jnp.concatenate(b_v, axis=0)
    seg_tot = b_v[-1]

    sub = lax.broadcasted_iota(jnp.int32, (SUBLANES, LANES), 0)
    inc = seg_tot
    for sh in (1, 2, 4):
        inc = inc + jnp.where(sub >= sh, pltpu.roll(inc, sh, 0), 0.0)
    off = inc - seg_tot
    total = inc[SUBLANES - 1:SUBLANES, :]
    bg = b_loc + jnp.concatenate([off] * SEG, axis=0)

    q_in = (qs * jnp.exp2(bg)).astype(BF16)
    o = lax.dot_general(q_in, state_t.astype(BF16), (((1,), (1,)), ((), ())),
                        preferred_element_type=F32)
    k_dec = (kk * jnp.exp2(total - bg)).astype(BF16)
    v_b = v.astype(BF16)
    new_state_t = state_t * jnp.exp2(total) + lax.dot_general(
        v_b, k_dec, (((0,), (0,)), ((), ())), preferred_element_type=F32)

    bounds = [off[(SUBLANES >> (d + 1)) * k:(SUBLANES >> (d + 1)) * k + 1, :] if k else jnp.zeros_like(total)
              for d in range(MERGE_DEPTHS) for k in range(2 << d)]
    span = []
    pos = 0
    for d in range(MERGE_DEPTHS):
        edges = bounds[pos:pos + (2 << d)] + [total]
        pos += 2 << d
        worst = edges[0] - edges[1]
        for k in range(1, 2 << d):
            worst = jnp.maximum(worst, edges[k] - edges[k + 1])
        span.append(worst)
    return _Head(qs=qs, kk=kk, v=v, v_b=v_b, b_v=b_v, b_loc=b_loc, off=off, sub=sub, o=o, span=span), new_state_t


def _pair_scores(head, x_q, x_k):
    return lax.dot_general((x_q * head.qs).astype(BF16), (x_k * head.kk).astype(BF16),
                           (((1,), (1,)), ((), ())), preferred_element_type=F32)


def _level_scores(head, lvl, masks_ref):
    kind, m = _level_plan()[lvl]
    h = m // 2
    if kind == "seg":
        row_sub = lax.broadcasted_iota(jnp.int32, (TILE, LANES), 0) & (SUBLANES - 1)
        t = head.b_loc + jnp.concatenate([_seg_split_offset(head.off, head.sub, m)] * SEG, axis=0)
        z = jnp.where((row_sub & (m - 1)) >= h, t, -t)
    else:
        b_v = head.b_v
        z_v = []
        for j in range(SEG):
            ref_v = b_v[(j // m) * m + h - 1]
            z_v.append(b_v[j] - ref_v if j % m >= h else ref_v - b_v[j])
        z = jnp.concatenate(z_v, axis=0)
    x = jnp.exp2(z)
    return masks_ref[lvl] * _pair_scores(head, x, x)


def _merged_scores(head, depth, masks_ref):
    t = head.b_loc + jnp.concatenate(
        [_seg_split_offset(head.off, head.sub, SUBLANES >> depth)] * SEG, axis=0)
    part = _pair_scores(head, jnp.exp2(t), jnp.exp2(-t))
    return jnp.where(masks_ref[len(_level_plan()) + depth] != 0.0, part, 0.0)


def _hgrn_scores(head, variant, masks_ref):
    n_levels = len(_level_plan())
    parts = [_level_scores(head, lvl, masks_ref) for lvl in range(min(variant, n_levels))]
    if variant < MERGE_DEPTHS:
        parts.append(_merged_scores(head, variant, masks_ref))
    scores = parts[0]
    for p in parts[1:]:
        scores = scores + p
    return scores


def _hgrn_finish(head, scores):
    o = head.o + jnp.dot(scores.astype(BF16), head.v_b, preferred_element_type=F32)
    return o + jnp.sum(head.qs * head.kk, axis=-1, keepdims=True) * head.v


def _mixer_kernel(xa_ref, xc_ref, win32_ref, poolw32_ref, pscale_ref, lb_ref, ng_ref, wout32_ref, lng_ref,
                  lnb_ref, masks_ref, o_ref, win_ref, poolw_ref, wout_ref, xab_ref, h_ref, mix_ref, pre_ref,
                  state_ref, tail_ref, *, alpha, tiles_per_seq):
    g = pl.program_id(0)

    @pl.when(g == 0)
    def _():
        win_ref[...] = win32_ref[...].astype(BF16)
        poolw_ref[...] = poolw32_ref[...].astype(BF16)
        wout_ref[...] = wout32_ref[...].astype(BF16)
        h_ref[...] = jnp.zeros_like(h_ref)
        mix_ref[...] = jnp.zeros_like(mix_ref)
        pre_ref[...] = jnp.zeros_like(pre_ref)
        state_ref[...] = jnp.zeros_like(state_ref)
        tail_ref[...] = jnp.zeros_like(tail_ref)

    tile_in_seq = (g + tiles_per_seq - 1) % tiles_per_seq
    first = tile_in_seq == 0

    for parity in range(2):
        pl.when(g % 2 == parity)(functools.partial(
            _mixer_step, parity, 1 - parity, first, tile_in_seq, xa_ref, xc_ref, win_ref, poolw_ref,
            pscale_ref, lb_ref, ng_ref, wout_ref, lng_ref, lnb_ref, masks_ref, o_ref, xab_ref, h_ref,
            mix_ref, pre_ref, state_ref, tail_ref, alpha=alpha))


def _mixer_step(slot_a, slot_b, first, tile_in_seq, xa_ref, xc_ref, win_ref, poolw_ref, pscale_ref, lb_ref,
                ng_ref, wout_ref, lng_ref, lnb_ref, masks_ref, o_ref, xab_ref, h_ref, mix_ref, pre_ref,
                state_ref, tail_ref, *, alpha):
    d_pool = LANES * len(POOL_WINDOWS)
    d_hgrn = LANES * HGRN_HEADS
    halo_rows = POOL_HALO * SUBLANES
    d_out = o_ref.shape[1]
    o_ref[...] = _layer_norm(pre_ref[...], lng_ref[...], lnb_ref[...])
    xab_ref[...] = xa_ref[...].astype(BF16)

    def project(part):
        cols = slice(part * d_hgrn, (part + 1) * d_hgrn)
        h_ref[slot_a, :, cols] = jnp.dot(xab_ref[...], win_ref[:, cols], preferred_element_type=F32)

    def finish(part):
        cols = slice(part * d_out // HGRN_HEADS, (part + 1) * d_out // HGRN_HEADS)
        pre_ref[:, cols] = alpha * xc_ref[:, cols] + jnp.dot(mix_ref[slot_a], wout_ref[:, cols],
                                                            preferred_element_type=F32)

    project(0)
    hb = h_ref.at[slot_b]

    u = hb[:, :d_pool]
    cur_tail = u[TILE - halo_rows:, :]
    sub = lax.broadcasted_iota(jnp.int32, cur_tail.shape, 0) & (SUBLANES - 1)
    prev_tail = jnp.where(first, 0.0, tail_ref[...])
    head = jnp.where(sub == 0,
                     pltpu.roll(prev_tail, halo_rows - (SUBLANES - 1), 0),
                     pltpu.roll(cur_tail, 1, 0))
    tail_ref[...] = cur_tail
    ext = jnp.concatenate([head, u], axis=0)
    for grp, window in enumerate(POOL_WINDOWS):
        cols = slice(grp * LANES, (grp + 1) * LANES)
        pooled = _pool_group(ext[:, cols], u[:, cols], window, tile_in_seq)
        y = jnp.dot(pooled.astype(BF16), poolw_ref[grp], preferred_element_type=F32)
        mix_ref[slot_b, :, cols] = (y * pscale_ref[:, cols]).astype(BF16)

    def part(hd, k):
        lo = d_pool + k * d_hgrn + hd * LANES
        return hb[:, lo:lo + LANES]

    heads = []
    for hd in range(HGRN_HEADS):
        if hd < 2:
            project(hd + 1)
        state = jnp.where(first, 0.0, state_ref[hd])
        head, new_state = _hgrn_prep(part(hd, 0), part(hd, 1), part(hd, 2),
                                     lb_ref[:, hd * LANES:(hd + 1) * LANES], state)
        state_ref[hd] = new_state
        heads.append(head)

    worst = [jnp.max(functools.reduce(jnp.maximum, [hd.span[d] for hd in heads])) for d in range(MERGE_DEPTHS)]

    def mix_heads(variant):
        for hd, head in enumerate(heads):
            if hd < 2:
                project(hd + 3)
            o = _hgrn_finish(head, _hgrn_scores(head, variant, masks_ref))
            o = o * lax.rsqrt(jnp.mean(o * o, axis=-1, keepdims=True) + RMS_EPS) * ng_ref[...]
            mix_ref[slot_b, :, d_pool + hd * LANES:d_pool + (hd + 1) * LANES] = (
                o * _silu(part(hd, 3))).astype(BF16)
            finish(hd)

    ok0 = worst[0] <= MERGE_LOG2_SPAN
    ok1 = worst[1] <= MERGE_LOG2_SPAN
    pl.when(ok0)(functools.partial(mix_heads, 0))
    pl.when(jnp.logical_not(ok0) & ok1)(functools.partial(mix_heads, 1))
    pl.when(jnp.logical_not(ok0 | ok1))(functools.partial(mix_heads, len(_level_plan())))


def _mixer(x3d, w_in, pool_w, pool_scale, lb, norm_g, w_out, ln_g, ln_b, masks, layer, *, alpha):
    bsz, seq, d = x3d.shape
    d_in = w_in.shape[2]
    d_mix = w_out.shape[1]
    assert seq % TILE == 0
    tiles_per_seq = seq // TILE
    n_tiles = bsz * tiles_per_seq
    n_lvl = masks.shape[0]
    x_tiles = x3d.reshape(n_tiles, TILE, d)
    last = n_tiles - 1
    vmem = (3 * 2 * TILE * d * 4
            + (4 + 2) * (d * d_in + d_mix * d + pool_w[0].size)
            + n_lvl * TILE * TILE * 4
            + 2 * TILE * d_in * 4 + 2 * TILE * d_mix * 2 + TILE * d * (2 + 4)
            + 4 * TILE * d_in * 4
            + 8 * TILE * TILE * 4)
    const = lambda *shape: _resident(shape, lambda g: (0,) * len(shape))
    at_layer = lambda *shape: _resident((None,) + shape, lambda g: (layer,) + (0,) * len(shape))
    out = pl.pallas_call(
        functools.partial(_mixer_kernel, alpha=alpha, tiles_per_seq=tiles_per_seq),
        grid=(n_tiles + 3,),
        in_specs=[
            pl.BlockSpec((None, TILE, d), lambda g: (jnp.minimum(g, last), 0, 0)),
            pl.BlockSpec((None, TILE, d), lambda g: (jnp.clip(g - 2, 0, last), 0, 0)),
            at_layer(d, d_in),
            at_layer(*pool_w.shape[1:]),
            const(1, pool_scale.shape[-1]),
            const(1, lb.shape[-1]),
            const(1, norm_g.shape[-1]),
            at_layer(d_mix, d),
            const(1, d),
            const(1, d),
            const(*masks.shape),
        ],
        out_specs=pl.BlockSpec((None, TILE, d), lambda g: (jnp.clip(g - 3, 0, last), 0, 0)),
        out_shape=jax.ShapeDtypeStruct((n_tiles, TILE, d), F32),
        scratch_shapes=[
            pltpu.VMEM((d, d_in), BF16), pltpu.VMEM(pool_w.shape[1:], BF16), pltpu.VMEM((d_mix, d), BF16),
            pltpu.VMEM((TILE, d), BF16),
            pltpu.VMEM((2, TILE, d_in), F32),
            pltpu.VMEM((2, TILE, d_mix), BF16),
            pltpu.VMEM((TILE, d), F32),
            pltpu.VMEM((HGRN_HEADS, LANES, LANES), F32),
            pltpu.VMEM((POOL_HALO * SUBLANES, LANES * len(POOL_WINDOWS)), F32),
        ],
        compiler_params=pltpu.CompilerParams(
            dimension_semantics=("arbitrary",),
            vmem_limit_bytes=int(vmem * 1.2)),
        name="mixer",
    )(x_tiles, x_tiles, w_in, pool_w, pool_scale.reshape(1, -1), lb.reshape(1, -1), norm_g.reshape(1, -1),
      w_out, ln_g.reshape(1, d), ln_b.reshape(1, d), masks)
    return out.reshape(bsz, seq, d)


def _interleave(x):
    bsz, seq, d = x.shape
    x = x.reshape(bsz, seq // TILE, SUBLANES, SEG, d)
    return jnp.swapaxes(x, 2, 3).reshape(bsz, seq, d)


def _deinterleave(x):
    bsz, seq, d = x.shape
    x = x.reshape(bsz, seq // TILE, SEG, SUBLANES, d)
    return jnp.swapaxes(x, 2, 3).reshape(bsz, seq, d)


def kernel(x, w_in, pool_w, pool_scale, lb_param, hgrn_norm_g, w_out, ffn1_gate, ffn1_up, ffn1_down,
           ffn2_gate, ffn2_up, ffn2_down, ln_g, ln_b):
    bsz, seq, d = x.shape
    depth = w_in.shape[0]
    alpha = (2.0 * depth) ** 0.25

    lb_all = jnp.cumsum(jax.nn.softmax(lb_param.astype(F32), axis=0), axis=0)
    lb_all = lb_all - lb_all[0:1]

    masks = jnp.asarray(_level_masks())

    xp = _interleave(x)
    for l in range(depth):
        x2 = _ffn(xp.reshape(bsz * seq, d), ffn1_gate, ffn1_up, ffn1_down, ln_g[l, 0], ln_b[l, 0], l,
                  alpha=alpha)
        xm = _mixer(x2.reshape(bsz, seq, d), w_in, pool_w, pool_scale[l], lb_all[l],
                    hgrn_norm_g[l], w_out, ln_g[l, 1], ln_b[l, 1], masks, l, alpha=alpha)
        x2 = _ffn(xm.reshape(bsz * seq, d), ffn2_gate, ffn2_up, ffn2_down, ln_g[l, 2], ln_b[l, 2], l,
                  alpha=alpha)
        xp = x2.reshape(bsz, seq, d)
    return _deinterleave(xp)
```

```python
import functools

import numpy as np
import jax
import jax.numpy as jnp
from jax import lax
from jax.experimental import pallas as pl
from jax.experimental.pallas import tpu as pltpu

F32 = jnp.float32
BF16 = jnp.bfloat16

SUBLANES = 8
LANES = 128
MXU_DIM = 256

POOL_WINDOWS = (2, 4, 8, 16)
HGRN_HEADS = 4
LN_EPS = 1e-5
RMS_EPS = 1e-6
LOG2E = 1.4426950408889634

TILE = 256
SEG = TILE // SUBLANES
FFN_ROWS = 512
FFN_COLS = MXU_DIM
FFN_LN_SLICES = 8
POOL_HALO = max(POOL_WINDOWS) - 1
MERGE_DEPTHS = 2
MERGE_LOG2_SPAN = 120.0


def _layer_norm(y, g, b):
    mu = jnp.mean(y, axis=-1, keepdims=True)
    yc = y - mu
    var = jnp.mean(yc * yc, axis=-1, keepdims=True)
    return yc * lax.rsqrt(var + LN_EPS) * g + b


def _silu(x):
    return x * jax.nn.sigmoid(x)


def _anchor_zero(*arrays):
    folded = None
    for a in arrays:
        bits = lax.bitcast_convert_type(a, jnp.uint32)
        for r in range(0, a.shape[0], SUBLANES):
            for l in range(0, a.shape[1], LANES):
                piece = bits[r:r + SUBLANES, l:l + LANES]
                folded = piece if folded is None else folded | piece
    return lax.bitcast_convert_type((folded >> 16) >> 16, F32)


def _resident(shape, index_map):
    return pl.BlockSpec(shape, index_map, pipeline_mode=pl.Buffered(1))


def _ffn_kernel(x_ref, wg_hbm, wu_hbm, wd_hbm, g_ref, b_ref, o_ref, wg_ref, wu_ref, wd_ref, stage_g, stage_u,
                stage_d, sems, xb_ref, hid_ref, acc_ref, *permute_refs, alpha, n_tiles, layer, permute_in,
                permute_out):
    i = pl.program_id(0)
    d = acc_ref.shape[2]
    dff = wg_ref.shape[1]
    n_chunks = dff // FFN_COLS
    cur = i % 2
    prev = 1 - cur
    ln_rows = FFN_ROWS // FFN_LN_SLICES
    permute_refs = list(permute_refs)
    xin_ref, sem_x = (permute_refs.pop(0), permute_refs.pop(0)) if permute_in else (None, None)
    out_ref, sem_o = (permute_refs.pop(0), permute_refs.pop(0)) if permute_out else (None, None)

    def segment_copies(tile, slot, to_vmem):
        hbm, vmem, sem = (x_ref, xin_ref, sem_x) if to_vmem else (o_ref, out_ref, sem_o)
        copies = []
        for m in range(FFN_ROWS // TILE):
            for seg in range(SUBLANES):
                there = hbm.at[pl.ds(tile * FFN_ROWS + m * TILE + seg * SEG, SEG), :]
                here = vmem.at[slot, m * SEG:(m + 1) * SEG, seg, :]
                src, dst = (there, here) if to_vmem else (here, there)
                copies.append(pltpu.make_async_copy(src, dst, sem.at[slot, m * SUBLANES + seg]))
        return copies

    def finish_slice(k):
        rows = slice(k * ln_rows, (k + 1) * ln_rows)
        y = _layer_norm(0.5 * acc_ref[prev, rows, :], g_ref[...], b_ref[...])
        if permute_out:
            vregs = ln_rows // SUBLANES
            out_ref[prev, k * vregs:(k + 1) * vregs] = y.reshape(vregs, SUBLANES, d)
        else:
            o_ref[rows, :] = y
        return _anchor_zero(y)

    def weight_copies(c):
        slot = c % 2
        cols = pl.ds(c * FFN_COLS, FFN_COLS)
        return (pltpu.make_async_copy(wg_hbm.at[layer, :, cols], stage_g.at[slot], sems.at[slot, 0]),
                pltpu.make_async_copy(wu_hbm.at[layer, :, cols], stage_u.at[slot], sems.at[slot, 1]),
                pltpu.make_async_copy(wd_hbm.at[layer, cols, :], stage_d.at[slot], sems.at[slot, 2]))

    @pl.when(i == 0)
    def _():
        acc_ref[1] = jnp.zeros(acc_ref.shape[1:], F32)
        if permute_in:
            for cp in segment_copies(0, 0, True):
                cp.start()
        for cp in weight_copies(0):
            cp.start()
        for c in range(n_chunks):
            if c + 1 < n_chunks:
                for cp in weight_copies(c + 1):
                    cp.start()
            for cp in weight_copies(c):
                cp.wait()
            slot = c % 2
            cols = slice(c * FFN_COLS, (c + 1) * FFN_COLS)
            wg_ref[:, cols] = stage_g[slot].astype(BF16)
            wu_ref[:, cols] = stage_u[slot].astype(BF16)
            wd_ref[cols, :] = stage_d[slot].astype(BF16)

    if permute_in:
        @pl.when(i + 1 < n_tiles)
        def _():
            for cp in segment_copies(i + 1, prev, True):
                cp.start()

    if permute_out:
        @pl.when(i >= 3)
        def _():
            for cp in segment_copies(i - 3, prev, False):
                cp.wait()

    @pl.when(i < n_tiles)
    def _():
        if permute_in:
            for cp in segment_copies(i, cur, True):
                cp.wait()
            x = xin_ref[cur].reshape(FFN_ROWS, d)
        else:
            x = x_ref[...]
        xb_ref[...] = x.astype(BF16)
        anchor = None
        for c in range(n_chunks):
            cols = slice(c * FFN_COLS, (c + 1) * FFN_COLS)
            xb = xb_ref[...]
            gate = jnp.dot(xb, wg_ref[:, cols], preferred_element_type=F32)
            up = jnp.dot(xb, wu_ref[:, cols], preferred_element_type=F32)
            if anchor is not None:
                zero = jnp.concatenate([anchor] * (FFN_COLS // LANES), axis=1)
                up = jnp.concatenate([up[:SUBLANES] + zero, up[SUBLANES:]], axis=0)
            hid_ref[:, cols] = (_silu(gate) * up).astype(BF16)
            anchor = finish_slice(c) if c < FFN_LN_SLICES else None
        acc_ref[cur] = (2.0 * alpha) * x + jnp.dot(hid_ref[...], wd_ref[...], preferred_element_type=F32)

    @pl.when(i == n_tiles)
    def _():
        for k in range(FFN_LN_SLICES):
            finish_slice(k)

    if permute_out:
        @pl.when(i >= 1)
        def _():
            for cp in segment_copies(i - 1, prev, False):
                cp.start()

        @pl.when(i == n_tiles)
        def _():
            for tile in (n_tiles - 2, n_tiles - 1):
                for cp in segment_copies(tile, tile % 2, False):
                    cp.wait()


def _ffn(x2d, wg, wu, wd, g, b, layer, *, alpha, permute_in=False, permute_out=False):
    rows, d = x2d.shape
    dff = wg.shape[2]
    assert rows % FFN_ROWS == 0 and dff % FFN_COLS == 0 and FFN_ROWS % TILE == 0
    assert FFN_LN_SLICES <= dff // FFN_COLS and FFN_ROWS % FFN_LN_SLICES == 0
    n_tiles = rows // FFN_ROWS
    assert n_tiles >= 3
    vmem = (3 * d * dff * 2
            + 2 * 3 * d * FFN_COLS * 4
            + 2 * 2 * FFN_ROWS * d * 4
            + FFN_ROWS * d * (2 + 2 * 4)
            + FFN_ROWS * dff * 2
            + 3 * FFN_ROWS * FFN_COLS * 4
            + FFN_ROWS * d * 4)
    in_hbm = pl.BlockSpec(memory_space=pl.ANY)
    permute_scratch = [pltpu.VMEM((2, FFN_ROWS // SUBLANES, SUBLANES, d), F32),
                       pltpu.SemaphoreType.DMA((2, FFN_ROWS // SEG))]
    return pl.pallas_call(
        functools.partial(_ffn_kernel, alpha=alpha, n_tiles=n_tiles, layer=layer, permute_in=permute_in,
                          permute_out=permute_out),
        grid=(n_tiles + 1,),
        in_specs=[
            in_hbm if permute_in else pl.BlockSpec((FFN_ROWS, d), lambda i: (jnp.minimum(i, n_tiles - 1), 0)),
            in_hbm, in_hbm, in_hbm,
            _resident((1, d), lambda i: (0, 0)),
            _resident((1, d), lambda i: (0, 0)),
        ],
        out_specs=in_hbm if permute_out else pl.BlockSpec((FFN_ROWS, d), lambda i: (jnp.maximum(i - 1, 0), 0)),
        out_shape=jax.ShapeDtypeStruct((rows, d), F32),
        scratch_shapes=[
            pltpu.VMEM((d, dff), BF16), pltpu.VMEM((d, dff), BF16), pltpu.VMEM((dff, d), BF16),
            pltpu.VMEM((2, d, FFN_COLS), F32), pltpu.VMEM((2, d, FFN_COLS), F32),
            pltpu.VMEM((2, FFN_COLS, d), F32), pltpu.SemaphoreType.DMA((2, 3)),
            pltpu.VMEM((FFN_ROWS, d), BF16), pltpu.VMEM((FFN_ROWS, dff), BF16),
            pltpu.VMEM((2, FFN_ROWS, d), F32)]
        + (permute_scratch if permute_in else []) + (permute_scratch if permute_out else []),
        compiler_params=pltpu.CompilerParams(
            dimension_semantics=("arbitrary",),
            vmem_limit_bytes=int(vmem * 1.1)),
        name="ffn",
    )(x2d, wg, wu, wd, g.reshape(1, d), b.reshape(1, d))


def _level_plan():
    plan = []
    m = SUBLANES
    while m >= 2:
        plan.append(("seg", m))
        m //= 2
    m = SEG
    while m >= 2:
        plan.append(("pos", m))
        m //= 2
    return plan


def _level_masks():
    r = np.arange(TILE)
    seg, pos = r % SUBLANES, r // SUBLANES
    masks = []
    for kind, m in _level_plan():
        h = m // 2
        if kind == "seg":
            same = (seg[:, None] // m) == (seg[None, :] // m)
            late, early = (seg % m) >= h, (seg % m) < h
        else:
            same = (seg[:, None] == seg[None, :]) & ((pos[:, None] // m) == (pos[None, :] // m))
            late, early = (pos % m) >= h, (pos % m) < h
        masks.append(same & late[:, None] & early[None, :])
    masks = np.stack(masks)
    time = seg * SEG + pos
    causal = time[None, :] < time[:, None]
    assert (masks.sum(0) == causal).all()
    merged = [causal & ((seg[:, None] >> (3 - d)) == (seg[None, :] >> (3 - d))) for d in range(MERGE_DEPTHS)]
    for d in range(MERGE_DEPTHS):
        assert (masks[:d].sum(0) + merged[d] == causal).all()
    return np.concatenate([masks, np.stack(merged)]).astype(np.float32)


def _pool_group(ext, u_g, window, tile_in_seq):
    cur = ext[(POOL_HALO - (window - 1)) * SUBLANES:]
    span = 1
    while span < window:
        cur = cur[span * SUBLANES:] + cur[:-span * SUBLANES]
        span *= 2
    row = lax.broadcasted_iota(jnp.int32, (TILE, LANES), 0)
    pos = tile_in_seq * TILE + (row & (SUBLANES - 1)) * SEG + (row >> 3)
    cnt = jnp.minimum(pos + 1, window).astype(F32)
    return cur / cnt - u_g


def _gates(f_raw, lb):
    t = jnp.exp(-jnp.abs(f_raw))
    log_sig = jnp.minimum(f_raw, 0.0) - jnp.log(1.0 + t)
    a = jnp.log(lb)
    c = jnp.log1p(-lb) + log_sig
    log_f = jnp.maximum(a, c) + jnp.log(1.0 + jnp.exp(-jnp.abs(a - c)))
    inv = 1.0 / (1.0 + t)
    sig_neg = jnp.where(f_raw >= 0.0, t * inv, inv)
    return log_f * LOG2E, (1.0 - lb) * sig_neg


def _vregs(a):
    return [a[j * SUBLANES:(j + 1) * SUBLANES] for j in range(a.shape[0] // SUBLANES)]


class _Head:
    def __init__(self, **kw):
        self.__dict__.update(kw)


def _seg_split_offset(off, sub, m):
    ref = off
    for blk in range(SUBLANES // m):
        src = blk * m + m // 2
        in_blk = (sub >= blk * m) & (sub < (blk + 1) * m)
        ref = jnp.where(in_blk, jnp.broadcast_to(off[src:src + 1, :], off.shape), ref)
    return off - ref


def _hgrn_prep(q, f_raw, v, lb, state_t):
    lf, kk = _gates(f_raw, lb)
    qs = q * (LANES ** -0.5)

    lf_v = _vregs(lf)
    b_v = [lf_v[0]]
    for j in range(1, SEG):
        b_v.append(b_v[-1] + lf_v[j])
    b_loc = jnp.concatenate(b_v, axis=0)
    seg_tot = b_v[-1]

    sub = lax.broadcasted_iota(jnp.int32, (SUBLANES, LANES), 0)
    inc = seg_tot
    for sh in (1, 2, 4):
        inc = inc + jnp.where(sub >= sh, pltpu.roll(inc, sh, 0), 0.0)
    off = inc - seg_tot
    total = inc[SUBLANES - 1:SUBLANES, :]
    bg = b_loc + jnp.concatenate([off] * SEG, axis=0)

    q_in = (qs * jnp.exp2(bg)).astype(BF16)
    o = lax.dot_general(q_in, state_t.astype(BF16), (((1,), (1,)), ((), ())),
                        preferred_element_type=F32)
    k_dec = (kk * jnp.exp2(total - bg)).astype(BF16)
    v_b = v.astype(BF16)
    new_state_t = state_t * jnp.exp2(total) + lax.dot_general(
        v_b, k_dec, (((0,), (0,)), ((), ())), preferred_element_type=F32)

    bounds = [off[(SUBLANES >> (d + 1)) * k:(SUBLANES >> (d + 1)) * k + 1, :] if k else jnp.zeros_like(total)
              for d in range(MERGE_DEPTHS) for k in range(2 << d)]
    span = []
    pos = 0
    for d in range(MERGE_DEPTHS):
        edges = bounds[pos:pos + (2 << d)] + [total]
        pos += 2 << d
        worst = edges[0] - edges[1]
        for k in range(1, 2 << d):
            worst = jnp.maximum(worst, edges[k] - edges[k + 1])
        span.append(worst)
    return _Head(qs=qs, kk=kk, v=v, v_b=v_b, b_v=b_v, b_loc=b_loc, off=off, sub=sub, o=o, span=span), new_state_t


def _pair_scores(head, x_q, x_k):
    return lax.dot_general((x_q * head.qs).astype(BF16), (x_k * head.kk).astype(BF16),
                           (((1,), (1,)), ((), ())), preferred_element_type=F32)


def _level_scores(head, lvl, masks_ref):
    kind, m = _level_plan()[lvl]
    h = m // 2
    if kind == "seg":
        row_sub = lax.broadcasted_iota(jnp.int32, (TILE, LANES), 0) & (SUBLANES - 1)
        t = head.b_loc + jnp.concatenate([_seg_split_offset(head.off, head.sub, m)] * SEG, axis=0)
        z = jnp.where((row_sub & (m - 1)) >= h, t, -t)
    else:
        b_v = head.b_v
        z_v = []
        for j in range(SEG):
            ref_v = b_v[(j // m) * m + h - 1]
            z_v.append(b_v[j] - ref_v if j % m >= h else ref_v - b_v[j])
        z = jnp.concatenate(z_v, axis=0)
    x = jnp.exp2(z)
    return masks_ref[lvl] * _pair_scores(head, x, x)


def _merged_scores(head, depth, masks_ref):
    t = head.b_loc + jnp.concatenate(
        [_seg_split_offset(head.off, head.sub, SUBLANES >> depth)] * SEG, axis=0)
    part = _pair_scores(head, jnp.exp2(t), jnp.exp2(-t))
    return jnp.where(masks_ref[len(_level_plan()) + depth] != 0.0, part, 0.0)


def _hgrn_scores(head, variant, masks_ref):
    n_levels = len(_level_plan())
    parts = [_level_scores(head, lvl, masks_ref) for lvl in range(min(variant, n_levels))]
    if variant < MERGE_DEPTHS:
        parts.append(_merged_scores(head, variant, masks_ref))
    scores = parts[0]
    for p in parts[1:]:
        scores = scores + p
    return scores


def _hgrn_finish(head, scores):
    o = head.o + jnp.dot(scores.astype(BF16), head.v_b, preferred_element_type=F32)
    return o + jnp.sum(head.qs * head.kk, axis=-1, keepdims=True) * head.v


def _mixer_kernel(xa_ref, xc_ref, win32_ref, poolw32_ref, pscale_ref, lb_ref, ng_ref, wout32_ref, lng_ref,
                  lnb_ref, masks_ref, o_ref, win_ref, poolw_ref, wout_ref, xab_ref, h_ref, mix_ref, pre_ref,
                  state_ref, tail_ref, *, alpha, tiles_per_seq):
    g = pl.program_id(0)

    @pl.when(g == 0)
    def _():
        win_ref[...] = win32_ref[...].astype(BF16)
        poolw_ref[...] = poolw32_ref[...].astype(BF16)
        wout_ref[...] = wout32_ref[...].astype(BF16)
        h_ref[...] = jnp.zeros_like(h_ref)
        mix_ref[...] = jnp.zeros_like(mix_ref)
        pre_ref[...] = jnp.zeros_like(pre_ref)
        state_ref[...] = jnp.zeros_like(state_ref)
        tail_ref[...] = jnp.zeros_like(tail_ref)

    tile_in_seq = (g + tiles_per_seq - 1) % tiles_per_seq
    first = tile_in_seq == 0

    for parity in range(2):
        pl.when(g % 2 == parity)(functools.partial(
            _mixer_step, parity, 1 - parity, first, tile_in_seq, xa_ref, xc_ref, win_ref, poolw_ref,
            pscale_ref, lb_ref, ng_ref, wout_ref, lng_ref, lnb_ref, masks_ref, o_ref, xab_ref, h_ref,
            mix_ref, pre_ref, state_ref, tail_ref, alpha=alpha))


def _mixer_step(slot_a, slot_b, first, tile_in_seq, xa_ref, xc_ref, win_ref, poolw_ref, pscale_ref, lb_ref,
                ng_ref, wout_ref, lng_ref, lnb_ref, masks_ref, o_ref, xab_ref, h_ref, mix_ref, pre_ref,
                state_ref, tail_ref, *, alpha):
    d_pool = LANES * len(POOL_WINDOWS)
    d_hgrn = LANES * HGRN_HEADS
    halo_rows = POOL_HALO * SUBLANES
    d_out = o_ref.shape[1]
    normed = _layer_norm(pre_ref[...], lng_ref[...], lnb_ref[...])
    o_ref[...] = normed
    ln_anchor = _anchor_zero(normed)
    xab_ref[...] = xa_ref[...].astype(BF16)

    def project(part):
        cols = slice(part * d_hgrn, (part + 1) * d_hgrn)
        h_ref[slot_a, :, cols] = jnp.dot(xab_ref[...], win_ref[:, cols], preferred_element_type=F32)

    def finish(part):
        cols = slice(part * d_out // HGRN_HEADS, (part + 1) * d_out // HGRN_HEADS)
        pre_ref[:, cols] = alpha * xc_ref[:, cols] + jnp.dot(mix_ref[slot_a], wout_ref[:, cols],
                                                            preferred_element_type=F32)

    def hold_projection_until(zero):
        z = jnp.concatenate([zero, zero], axis=0).astype(BF16)
        xab_ref[0:2 * SUBLANES, 0:LANES] = xab_ref[0:2 * SUBLANES, 0:LANES] + z

    project(0)
    hb = h_ref.at[slot_b]

    u = hb[:, :d_pool]
    cur_tail = u[TILE - halo_rows:, :]
    sub = lax.broadcasted_iota(jnp.int32, cur_tail.shape, 0) & (SUBLANES - 1)
    prev_tail = jnp.where(first, 0.0, tail_ref[...])
    head = jnp.where(sub == 0,
                     pltpu.roll(prev_tail, halo_rows - (SUBLANES - 1), 0),
                     pltpu.roll(cur_tail, 1, 0))
    tail_ref[...] = cur_tail
    ext = jnp.concatenate([head, u], axis=0)
    pooled_out = []
    for grp, window in enumerate(POOL_WINDOWS):
        cols = slice(grp * LANES, (grp + 1) * LANES)
        pooled = _pool_group(ext[:, cols], u[:, cols], window, tile_in_seq)
        y = jnp.dot(pooled.astype(BF16), poolw_ref[grp], preferred_element_type=F32) * pscale_ref[:, cols]
        mix_ref[slot_b, :, cols] = y.astype(BF16)
        pooled_out.append(y)
    anchors = [ln_anchor, _anchor_zero(*pooled_out)]

    def part(hd, k):
        lo = d_pool + k * d_hgrn + hd * LANES
        return hb[:, lo:lo + LANES]

    heads = []
    for hd in range(HGRN_HEADS):
        hold_projection_until(anchors[hd])
        project(hd + 1)
        state = jnp.where(first, 0.0, state_ref[hd])
        head, new_state = _hgrn_prep(part(hd, 0), part(hd, 1), part(hd, 2),
                                     lb_ref[:, hd * LANES:(hd + 1) * LANES], state)
        state_ref[hd] = new_state
        heads.append(head)
        anchors.append(_anchor_zero(head.kk, head.b_loc))

    worst = [jnp.max(functools.reduce(jnp.maximum, [hd.span[d] for hd in heads])) for d in range(MERGE_DEPTHS)]

    def mix_heads(variant):
        for hd, head in enumerate(heads):
            o = _hgrn_finish(head, _hgrn_scores(head, variant, masks_ref))
            o = o * lax.rsqrt(jnp.mean(o * o, axis=-1, keepdims=True) + RMS_EPS) * ng_ref[...]
            mix_ref[slot_b, :, d_pool + hd * LANES:d_pool + (hd + 1) * LANES] = (
                o * _silu(part(hd, 3))).astype(BF16)
            finish(hd)

    ok0 = worst[0] <= MERGE_LOG2_SPAN
    ok1 = worst[1] <= MERGE_LOG2_SPAN
    pl.when(ok0)(functools.partial(mix_heads, 0))
    pl.when(jnp.logical_not(ok0) & ok1)(functools.partial(mix_heads, 1))
    pl.when(jnp.logical_not(ok0 | ok1))(functools.partial(mix_heads, len(_level_plan())))


def _mixer(x3d, w_in, pool_w, pool_scale, lb, norm_g, w_out, ln_g, ln_b, masks, layer, *, alpha):
    bsz, seq, d = x3d.shape
    d_in = w_in.shape[2]
    d_mix = w_out.shape[1]
    assert seq % TILE == 0
    tiles_per_seq = seq // TILE
    n_tiles = bsz * tiles_per_seq
    n_lvl = masks.shape[0]
    x_tiles = x3d.reshape(n_tiles, TILE, d)
    last = n_tiles - 1
    vmem = (3 * 2 * TILE * d * 4
            + (4 + 2) * (d * d_in + d_mix * d + pool_w[0].size)
            + n_lvl * TILE * TILE * 4
            + 2 * TILE * d_in * 4 + 2 * TILE * d_mix * 2 + TILE * d * (2 + 4)
            + 4 * TILE * d_in * 4
            + 8 * TILE * TILE * 4)
    const = lambda *shape: _resident(shape, lambda g: (0,) * len(shape))
    at_layer = lambda *shape: _resident((None,) + shape, lambda g: (layer,) + (0,) * len(shape))
    out = pl.pallas_call(
        functools.partial(_mixer_kernel, alpha=alpha, tiles_per_seq=tiles_per_seq),
        grid=(n_tiles + 3,),
        in_specs=[
            pl.BlockSpec((None, TILE, d), lambda g: (jnp.minimum(g, last), 0, 0)),
            pl.BlockSpec((None, TILE, d), lambda g: (jnp.clip(g - 2, 0, last), 0, 0)),
            at_layer(d, d_in),
            at_layer(*pool_w.shape[1:]),
            const(1, pool_scale.shape[-1]),
            const(1, lb.shape[-1]),
            const(1, norm_g.shape[-1]),
            at_layer(d_mix, d),
            const(1, d),
            const(1, d),
            const(*masks.shape),
        ],
        out_specs=pl.BlockSpec((None, TILE, d), lambda g: (jnp.clip(g - 3, 0, last), 0, 0)),
        out_shape=jax.ShapeDtypeStruct((n_tiles, TILE, d), F32),
        scratch_shapes=[
            pltpu.VMEM((d, d_in), BF16), pltpu.VMEM(pool_w.shape[1:], BF16), pltpu.VMEM((d_mix, d), BF16),
            pltpu.VMEM((TILE, d), BF16),
            pltpu.VMEM((2, TILE, d_in), F32),
            pltpu.VMEM((2, TILE, d_mix), BF16),
            pltpu.VMEM((TILE, d), F32),
            pltpu.VMEM((HGRN_HEADS, LANES, LANES), F32),
            pltpu.VMEM((POOL_HALO * SUBLANES, LANES * len(POOL_WINDOWS)), F32),
        ],
        compiler_params=pltpu.CompilerParams(
            dimension_semantics=("arbitrary",),
            vmem_limit_bytes=int(vmem * 1.2)),
        name="mixer",
    )(x_tiles, x_tiles, w_in, pool_w, pool_scale.reshape(1, -1), lb.reshape(1, -1), norm_g.reshape(1, -1),
      w_out, ln_g.reshape(1, d), ln_b.reshape(1, d), masks)
    return out.reshape(bsz, seq, d)


def kernel(x, w_in, pool_w, pool_scale, lb_param, hgrn_norm_g, w_out, ffn1_gate, ffn1_up, ffn1_down,
           ffn2_gate, ffn2_up, ffn2_down, ln_g, ln_b):
    bsz, seq, d = x.shape
    depth = w_in.shape[0]
    alpha = (2.0 * depth) ** 0.25

    lb_all = jnp.cumsum(jax.nn.softmax(lb_param.astype(F32), axis=0), axis=0)
    lb_all = lb_all - lb_all[0:1]

    masks = jnp.asarray(_level_masks())

    x2 = x.reshape(bsz * seq, d)
    for l in range(depth):
        x2 = _ffn(x2, ffn1_gate, ffn1_up, ffn1_down, ln_g[l, 0], ln_b[l, 0], l, alpha=alpha,
                  permute_in=(l == 0))
        xm = _mixer(x2.reshape(bsz, seq, d), w_in, pool_w, pool_scale[l], lb_all[l],
                    hgrn_norm_g[l], w_out, ln_g[l, 1], ln_b[l, 1], masks, l, alpha=alpha)
        x2 = _ffn(xm.reshape(bsz * seq, d), ffn2_gate, ffn2_up, ffn2_down, ln_g[l, 2], ln_b[l, 2], l,
                  alpha=alpha, permute_out=(l == depth - 1))
    return x2.reshape(bsz, seq, d)
```

```python
import functools

import numpy as np
import jax
import jax.numpy as jnp
from jax import lax
from jax.experimental import pallas as pl
from jax.experimental.pallas import tpu as pltpu

F32 = jnp.float32
BF16 = jnp.bfloat16

SUBLANES = 8
LANES = 128
MXU_DIM = 256

POOL_WINDOWS = (2, 4, 8, 16)
HGRN_HEADS = 4
LN_EPS = 1e-5
RMS_EPS = 1e-6
LOG2E = 1.4426950408889634

TILE = 256
SEG = TILE // SUBLANES
FFN_ROWS = 512
FFN_COLS = MXU_DIM
FFN_LN_SLICES = 8
POOL_HALO = max(POOL_WINDOWS) - 1
MERGE_DEPTHS = 2
MERGE_LOG2_SPAN = 240.0


def _layer_norm(y, g, b):
    mu = jnp.mean(y, axis=-1, keepdims=True)
    yc = y - mu
    var = jnp.mean(yc * yc, axis=-1, keepdims=True)
    return yc * lax.rsqrt(var + LN_EPS) * g + b


def _silu(x):
    return x * jax.nn.sigmoid(x)


def _anchor_zero(*arrays):
    folded = None
    for a in arrays:
        bits = lax.bitcast_convert_type(a, jnp.uint32)
        for r in range(0, a.shape[0], SUBLANES):
            for l in range(0, a.shape[1], LANES):
                piece = bits[r:r + SUBLANES, l:l + LANES]
                folded = piece if folded is None else folded | piece
    return lax.bitcast_convert_type((folded >> 16) >> 16, F32)


def _resident(shape, index_map):
    return pl.BlockSpec(shape, index_map, pipeline_mode=pl.Buffered(1))


def _ffn_kernel(x_ref, wg_hbm, wu_hbm, wd_hbm, g_ref, b_ref, o_ref, wg_ref, wu_ref, wd_ref, stage_g, stage_u,
                stage_d, sems, xb_ref, hid_ref, acc_ref, *permute_refs, alpha, n_tiles, layer, permute_in,
                permute_out):
    i = pl.program_id(0)
    d = acc_ref.shape[2]
    dff = wg_ref.shape[1]
    n_chunks = dff // FFN_COLS
    cur = i % 2
    prev = 1 - cur
    ln_rows = FFN_ROWS // FFN_LN_SLICES
    permute_refs = list(permute_refs)
    xin_ref, sem_x = (permute_refs.pop(0), permute_refs.pop(0)) if permute_in else (None, None)
    out_ref, sem_o = (permute_refs.pop(0), permute_refs.pop(0)) if permute_out else (None, None)

    def segment_copies(tile, slot, to_vmem):
        hbm, vmem, sem = (x_ref, xin_ref, sem_x) if to_vmem else (o_ref, out_ref, sem_o)
        copies = []
        for m in range(FFN_ROWS // TILE):
            for seg in range(SUBLANES):
                there = hbm.at[pl.ds(tile * FFN_ROWS + m * TILE + seg * SEG, SEG), :]
                here = vmem.at[slot, m * SEG:(m + 1) * SEG, seg, :]
                src, dst = (there, here) if to_vmem else (here, there)
                copies.append(pltpu.make_async_copy(src, dst, sem.at[slot, m * SUBLANES + seg]))
        return copies

    def finish_slice(k):
        rows = slice(k * ln_rows, (k + 1) * ln_rows)
        y = _layer_norm(0.5 * acc_ref[prev, rows, :], g_ref[...], b_ref[...])
        if permute_out:
            vregs = ln_rows // SUBLANES
            out_ref[prev, k * vregs:(k + 1) * vregs] = y.reshape(vregs, SUBLANES, d)
        else:
            o_ref[rows, :] = y
        return _anchor_zero(y)

    def weight_copies(c):
        slot = c % 2
        cols = pl.ds(c * FFN_COLS, FFN_COLS)
        return (pltpu.make_async_copy(wg_hbm.at[layer, :, cols], stage_g.at[slot], sems.at[slot, 0]),
                pltpu.make_async_copy(wu_hbm.at[layer, :, cols], stage_u.at[slot], sems.at[slot, 1]),
                pltpu.make_async_copy(wd_hbm.at[layer, cols, :], stage_d.at[slot], sems.at[slot, 2]))

    @pl.when(i == 0)
    def _():
        acc_ref[1] = jnp.zeros(acc_ref.shape[1:], F32)
        if permute_in:
            for cp in segment_copies(0, 0, True):
                cp.start()
        for cp in weight_copies(0):
            cp.start()
        for c in range(n_chunks):
            if c + 1 < n_chunks:
                for cp in weight_copies(c + 1):
                    cp.start()
            for cp in weight_copies(c):
                cp.wait()
            slot = c % 2
            cols = slice(c * FFN_COLS, (c + 1) * FFN_COLS)
            wg_ref[:, cols] = stage_g[slot].astype(BF16)
            wu_ref[:, cols] = stage_u[slot].astype(BF16)
            wd_ref[cols, :] = stage_d[slot].astype(BF16)

    if permute_in:
        @pl.when(i + 1 < n_tiles)
        def _():
            for cp in segment_copies(i + 1, prev, True):
                cp.start()

    if permute_out:
        @pl.when(i >= 3)
        def _():
            for cp in segment_copies(i - 3, prev, False):
                cp.wait()

    @pl.when(i < n_tiles)
    def _():
        if permute_in:
            for cp in segment_copies(i, cur, True):
                cp.wait()
            x = xin_ref[cur].reshape(FFN_ROWS, d)
        else:
            x = x_ref[...]
        xb_ref[...] = x.astype(BF16)
        anchor = None
        for c in range(n_chunks):
            cols = slice(c * FFN_COLS, (c + 1) * FFN_COLS)
            xb = xb_ref[...]
            gate = jnp.dot(xb, wg_ref[:, cols], preferred_element_type=F32)
            up = jnp.dot(xb, wu_ref[:, cols], preferred_element_type=F32)
            if anchor is not None:
                zero = jnp.concatenate([anchor] * (FFN_COLS // LANES), axis=1)
                up = jnp.concatenate([up[:SUBLANES] + zero, up[SUBLANES:]], axis=0)
            hid_ref[:, cols] = (_silu(gate) * up).astype(BF16)
            anchor = finish_slice(c) if c < FFN_LN_SLICES else None
        acc_ref[cur] = (2.0 * alpha) * x + jnp.dot(hid_ref[...], wd_ref[...], preferred_element_type=F32)

    @pl.when(i == n_tiles)
    def _():
        for k in range(FFN_LN_SLICES):
            finish_slice(k)

    if permute_out:
        @pl.when(i >= 1)
        def _():
            for cp in segment_copies(i - 1, prev, False):
                cp.start()

        @pl.when(i == n_tiles)
        def _():
            for tile in (n_tiles - 2, n_tiles - 1):
                for cp in segment_copies(tile, tile % 2, False):
                    cp.wait()


def _ffn(x2d, wg, wu, wd, g, b, layer, *, alpha, permute_in=False, permute_out=False):
    rows, d = x2d.shape
    dff = wg.shape[2]
    assert rows % FFN_ROWS == 0 and dff % FFN_COLS == 0 and FFN_ROWS % TILE == 0
    assert FFN_LN_SLICES <= dff // FFN_COLS and FFN_ROWS % FFN_LN_SLICES == 0
    n_tiles = rows // FFN_ROWS
    assert n_tiles >= 3
    vmem = (3 * d * dff * 2
            + 2 * 3 * d * FFN_COLS * 4
            + 2 * 2 * FFN_ROWS * d * 4
            + FFN_ROWS * d * (2 + 2 * 4)
            + FFN_ROWS * dff * 2
            + 3 * FFN_ROWS * FFN_COLS * 4
            + FFN_ROWS * d * 4)
    in_hbm = pl.BlockSpec(memory_space=pl.ANY)
    permute_scratch = [pltpu.VMEM((2, FFN_ROWS // SUBLANES, SUBLANES, d), F32),
                       pltpu.SemaphoreType.DMA((2, FFN_ROWS // SEG))]
    return pl.pallas_call(
        functools.partial(_ffn_kernel, alpha=alpha, n_tiles=n_tiles, layer=layer, permute_in=permute_in,
                          permute_out=permute_out),
        grid=(n_tiles + 1,),
        in_specs=[
            in_hbm if permute_in else pl.BlockSpec((FFN_ROWS, d), lambda i: (jnp.minimum(i, n_tiles - 1), 0)),
            in_hbm, in_hbm, in_hbm,
            _resident((1, d), lambda i: (0, 0)),
            _resident((1, d), lambda i: (0, 0)),
        ],
        out_specs=in_hbm if permute_out else pl.BlockSpec((FFN_ROWS, d), lambda i: (jnp.maximum(i - 1, 0), 0)),
        out_shape=jax.ShapeDtypeStruct((rows, d), F32),
        scratch_shapes=[
            pltpu.VMEM((d, dff), BF16), pltpu.VMEM((d, dff), BF16), pltpu.VMEM((dff, d), BF16),
            pltpu.VMEM((2, d, FFN_COLS), F32), pltpu.VMEM((2, d, FFN_COLS), F32),
            pltpu.VMEM((2, FFN_COLS, d), F32), pltpu.SemaphoreType.DMA((2, 3)),
            pltpu.VMEM((FFN_ROWS, d), BF16), pltpu.VMEM((FFN_ROWS, dff), BF16),
            pltpu.VMEM((2, FFN_ROWS, d), F32)]
        + (permute_scratch if permute_in else []) + (permute_scratch if permute_out else []),
        compiler_params=pltpu.CompilerParams(
            dimension_semantics=("arbitrary",),
            vmem_limit_bytes=int(vmem * 1.1)),
        name="ffn",
    )(x2d, wg, wu, wd, g.reshape(1, d), b.reshape(1, d))


def _level_plan():
    plan = []
    m = SUBLANES
    while m >= 2:
        plan.append(("seg", m))
        m //= 2
    m = SEG
    while m >= 2:
        plan.append(("pos", m))
        m //= 2
    return plan


def _level_masks():
    r = np.arange(TILE)
    seg, pos = r % SUBLANES, r // SUBLANES
    masks = []
    for kind, m in _level_plan():
        h = m // 2
        if kind == "seg":
            same = (seg[:, None] // m) == (seg[None, :] // m)
            late, early = (seg % m) >= h, (seg % m) < h
        else:
            same = (seg[:, None] == seg[None, :]) & ((pos[:, None] // m) == (pos[None, :] // m))
            late, early = (pos % m) >= h, (pos % m) < h
        masks.append(same & late[:, None] & early[None, :])
    masks = np.stack(masks)
    time = seg * SEG + pos
    causal = time[None, :] < time[:, None]
    assert (masks.sum(0) == causal).all()
    merged = [causal & ((seg[:, None] >> (3 - d)) == (seg[None, :] >> (3 - d))) for d in range(MERGE_DEPTHS)]
    for d in range(MERGE_DEPTHS):
        assert (masks[:d].sum(0) + merged[d] == causal).all()
    return np.concatenate([masks, np.stack(merged)]).astype(np.float32)


def _pool_group(ext, u_g, window, tile_in_seq):
    cur = ext[(POOL_HALO - (window - 1)) * SUBLANES:]
    span = 1
    while span < window:
        cur = cur[span * SUBLANES:] + cur[:-span * SUBLANES]
        span *= 2
    row = lax.broadcasted_iota(jnp.int32, (TILE, LANES), 0)
    pos = tile_in_seq * TILE + (row & (SUBLANES - 1)) * SEG + (row >> 3)
    cnt = jnp.minimum(pos + 1, window).astype(F32)
    return cur / cnt - u_g


def _gates(f_raw, lb):
    t = jnp.exp(-jnp.abs(f_raw))
    log_sig = jnp.minimum(f_raw, 0.0) - jnp.log(1.0 + t)
    a = jnp.log(lb)
    c = jnp.log1p(-lb) + log_sig
    log_f = jnp.maximum(a, c) + jnp.log(1.0 + jnp.exp(-jnp.abs(a - c)))
    inv = 1.0 / (1.0 + t)
    sig_neg = jnp.where(f_raw >= 0.0, t * inv, inv)
    return log_f * LOG2E, (1.0 - lb) * sig_neg


def _vregs(a):
    return [a[j * SUBLANES:(j + 1) * SUBLANES] for j in range(a.shape[0] // SUBLANES)]


class _Head:
    def __init__(self, **kw):
        self.__dict__.update(kw)


def _seg_split_offset(off, sub, m):
    ref = off
    for blk in range(SUBLANES // m):
        src = blk * m + m // 2
        in_blk = (sub >= blk * m) & (sub < (blk + 1) * m)
        ref = jnp.where(in_blk, jnp.broadcast_to(off[src:src + 1, :], off.shape), ref)
    return off - ref


def _hgrn_decay(f_raw, lb):
    lf, kk = _gates(f_raw, lb)

    lf_v = _vregs(lf)
    b_v = [lf_v[0]]
    for j in range(1, SEG):
        b_v.append(b_v[-1] + lf_v[j])
    b_loc = jnp.concatenate(b_v, axis=0)
    seg_tot = b_v[-1]

    sub = lax.broadcasted_iota(jnp.int32, (SUBLANES, LANES), 0)
    inc = seg_tot
    for sh in (1, 2, 4):
        inc = inc + jnp.where(sub >= sh, pltpu.roll(inc, sh, 0), 0.0)
    off = inc - seg_tot
    total = inc[SUBLANES - 1:SUBLANES, :]
    bg = b_loc + jnp.concatenate([off] * SEG, axis=0)

    span, center = [], []
    for d in range(MERGE_DEPTHS):
        segs = SUBLANES >> d
        edges = ([jnp.zeros_like(total)] + [off[segs * k:segs * k + 1, :] for k in range(1, 1 << d)]
                 + [total])
        worst, mid = None, None
        for k in range(1 << d):
            drop = edges[k] - edges[k + 1]
            worst = drop if worst is None else jnp.maximum(worst, drop)
            row = jnp.broadcast_to(0.5 * (edges[k] + edges[k + 1]), off.shape)
            mid = row if mid is None else jnp.where(sub >= segs * k, row, mid)
        span.append(worst)
        center.append(mid)
    return _Head(kk=kk, b_v=b_v, b_loc=b_loc, off=off, sub=sub, total=total, bg=bg, span=span,
                 center=center)


def _hgrn_state(head, q, v, state_t):
    qs = q * (LANES ** -0.5)
    q_in = (qs * jnp.exp2(head.bg)).astype(BF16)
    o = lax.dot_general(q_in, state_t.astype(BF16), (((1,), (1,)), ((), ())),
                        preferred_element_type=F32)
    k_dec = (head.kk * jnp.exp2(head.total - head.bg)).astype(BF16)
    v_b = v.astype(BF16)
    head.__dict__.update(qs=qs, v=v, v_b=v_b, o=o)
    return state_t * jnp.exp2(head.total) + lax.dot_general(
        v_b, k_dec, (((0,), (0,)), ((), ())), preferred_element_type=F32)


def _pair_scores(head, x_q, x_k):
    return lax.dot_general((x_q * head.qs).astype(BF16), (x_k * head.kk).astype(BF16),
                           (((1,), (1,)), ((), ())), preferred_element_type=F32)


def _level_scores(head, lvl, masks_ref):
    kind, m = _level_plan()[lvl]
    h = m // 2
    if kind == "seg":
        row_sub = lax.broadcasted_iota(jnp.int32, (TILE, LANES), 0) & (SUBLANES - 1)
        t = head.b_loc + jnp.concatenate([_seg_split_offset(head.off, head.sub, m)] * SEG, axis=0)
        z = jnp.where((row_sub & (m - 1)) >= h, t, -t)
    else:
        b_v = head.b_v
        z_v = []
        for j in range(SEG):
            ref_v = b_v[(j // m) * m + h - 1]
            z_v.append(b_v[j] - ref_v if j % m >= h else ref_v - b_v[j])
        z = jnp.concatenate(z_v, axis=0)
    x = jnp.exp2(z)
    return masks_ref[lvl] * _pair_scores(head, x, x)


def _merged_scores(head, depth, masks_ref):
    t = head.bg - jnp.concatenate([head.center[depth]] * SEG, axis=0)
    part = _pair_scores(head, jnp.exp2(t), jnp.exp2(-t))
    return jnp.where(masks_ref[len(_level_plan()) + depth] != 0.0, part, 0.0)


def _hgrn_scores(head, variant, masks_ref):
    n_levels = len(_level_plan())
    parts = [_level_scores(head, lvl, masks_ref) for lvl in range(min(variant, n_levels))]
    if variant < MERGE_DEPTHS:
        parts.append(_merged_scores(head, variant, masks_ref))
    scores = parts[0]
    for p in parts[1:]:
        scores = scores + p
    return scores


def _hgrn_finish(head, scores):
    o = head.o + jnp.dot(scores.astype(BF16), head.v_b, preferred_element_type=F32)
    return o + jnp.sum(head.qs * head.kk, axis=-1, keepdims=True) * head.v


def _mixer_kernel(xa_ref, xc_ref, win32_ref, poolw32_ref, pscale_ref, lb_ref, ng_ref, wout32_ref, lng_ref,
                  lnb_ref, masks_ref, o_ref, win_ref, poolw_ref, wout_ref, xab_ref, h_ref, mix_ref, pre_ref,
                  state_ref, tail_ref, *, alpha, tiles_per_seq):
    g = pl.program_id(0)

    @pl.when(g == 0)
    def _():
        win_ref[...] = win32_ref[...].astype(BF16)
        poolw_ref[...] = poolw32_ref[...].astype(BF16)
        wout_ref[...] = wout32_ref[...].astype(BF16)
        h_ref[...] = jnp.zeros_like(h_ref)
        mix_ref[...] = jnp.zeros_like(mix_ref)
        pre_ref[...] = jnp.zeros_like(pre_ref)
        state_ref[...] = jnp.zeros_like(state_ref)
        tail_ref[...] = jnp.zeros_like(tail_ref)

    tile_in_seq = (g + tiles_per_seq - 1) % tiles_per_seq
    first = tile_in_seq == 0

    for parity in range(2):
        pl.when(g % 2 == parity)(functools.partial(
            _mixer_step, parity, 1 - parity, first, tile_in_seq, xa_ref, xc_ref, win_ref, poolw_ref,
            pscale_ref, lb_ref, ng_ref, wout_ref, lng_ref, lnb_ref, masks_ref, o_ref, xab_ref, h_ref,
            mix_ref, pre_ref, state_ref, tail_ref, alpha=alpha))


def _mixer_step(slot_a, slot_b, first, tile_in_seq, xa_ref, xc_ref, win_ref, poolw_ref, pscale_ref, lb_ref,
                ng_ref, wout_ref, lng_ref, lnb_ref, masks_ref, o_ref, xab_ref, h_ref, mix_ref, pre_ref,
                state_ref, tail_ref, *, alpha):
    d_pool = LANES * len(POOL_WINDOWS)
    d_hgrn = LANES * HGRN_HEADS
    halo_rows = POOL_HALO * SUBLANES
    d_out = o_ref.shape[1]
    normed = _layer_norm(pre_ref[...], lng_ref[...], lnb_ref[...])
    o_ref[...] = normed
    ln_anchor = _anchor_zero(normed)
    xab_ref[...] = xa_ref[...].astype(BF16)

    def project(part):
        cols = slice(part * d_hgrn, (part + 1) * d_hgrn)
        h_ref[slot_a, :, cols] = jnp.dot(xab_ref[...], win_ref[:, cols], preferred_element_type=F32)

    def finish(part):
        cols = slice(part * d_out // HGRN_HEADS, (part + 1) * d_out // HGRN_HEADS)
        pre_ref[:, cols] = alpha * xc_ref[:, cols] + jnp.dot(mix_ref[slot_a], wout_ref[:, cols],
                                                            preferred_element_type=F32)

    def hold_projection_until(zero):
        z = jnp.concatenate([zero, zero], axis=0).astype(BF16)
        xab_ref[0:2 * SUBLANES, 0:LANES] = xab_ref[0:2 * SUBLANES, 0:LANES] + z

    project(0)
    hb = h_ref.at[slot_b]

    u = hb[:, :d_pool]
    cur_tail = u[TILE - halo_rows:, :]
    sub = lax.broadcasted_iota(jnp.int32, cur_tail.shape, 0) & (SUBLANES - 1)
    prev_tail = jnp.where(first, 0.0, tail_ref[...])
    head = jnp.where(sub == 0,
                     pltpu.roll(prev_tail, halo_rows - (SUBLANES - 1), 0),
                     pltpu.roll(cur_tail, 1, 0))
    tail_ref[...] = cur_tail
    ext = jnp.concatenate([head, u], axis=0)
    pooled_out = []
    for grp, window in enumerate(POOL_WINDOWS):
        cols = slice(grp * LANES, (grp + 1) * LANES)
        pooled = _pool_group(ext[:, cols], u[:, cols], window, tile_in_seq)
        y = jnp.dot(pooled.astype(BF16), poolw_ref[grp], preferred_element_type=F32) * pscale_ref[:, cols]
        mix_ref[slot_b, :, cols] = y.astype(BF16)
        pooled_out.append(y)
    anchors = [ln_anchor, _anchor_zero(*pooled_out)]

    def part(hd, k):
        lo = d_pool + k * d_hgrn + hd * LANES
        return hb[:, lo:lo + LANES]

    heads = []
    for hd in range(HGRN_HEADS):
        if hd % 2 == 0:
            hold_projection_until(anchors[hd // 2])
            project(hd // 2 + 1)
        heads.append(_hgrn_decay(part(hd, 1), lb_ref[:, hd * LANES:(hd + 1) * LANES]))
        if hd % 2 == 1:
            anchors.append(_anchor_zero(heads[hd - 1].bg, heads[hd].bg))

    spans = [functools.reduce(jnp.maximum, [hd.span[d] for hd in heads]) for d in range(MERGE_DEPTHS)]
    need = jnp.where(spans[0] <= MERGE_LOG2_SPAN, 0.0, jnp.where(spans[1] <= MERGE_LOG2_SPAN, 1.0, 2.0))
    level = jnp.max(need)

    for hd, head in enumerate(heads):
        if hd % 2 == 0:
            hold_projection_until(anchors[2 + hd // 2])
            project(hd // 2 + 3)
        state = jnp.where(first, 0.0, state_ref[hd])
        state_ref[hd] = _hgrn_state(head, part(hd, 0), part(hd, 2), state)

    def mix_heads(variant):
        for hd, head in enumerate(heads):
            o = _hgrn_finish(head, _hgrn_scores(head, variant, masks_ref))
            o = o * lax.rsqrt(jnp.mean(o * o, axis=-1, keepdims=True) + RMS_EPS) * ng_ref[...]
            mix_ref[slot_b, :, d_pool + hd * LANES:d_pool + (hd + 1) * LANES] = (
                o * _silu(part(hd, 3))).astype(BF16)
            finish(hd)

    pl.when(level == 0.0)(functools.partial(mix_heads, 0))
    pl.when(level == 1.0)(functools.partial(mix_heads, 1))
    pl.when(jnp.logical_not(level <= 1.0))(functools.partial(mix_heads, len(_level_plan())))


def _mixer(x3d, w_in, pool_w, pool_scale, lb, norm_g, w_out, ln_g, ln_b, masks, layer, *, alpha):
    bsz, seq, d = x3d.shape
    d_in = w_in.shape[2]
    d_mix = w_out.shape[1]
    assert seq % TILE == 0
    tiles_per_seq = seq // TILE
    n_tiles = bsz * tiles_per_seq
    n_lvl = masks.shape[0]
    x_tiles = x3d.reshape(n_tiles, TILE, d)
    last = n_tiles - 1
    vmem = (3 * 2 * TILE * d * 4
            + (4 + 2) * (d * d_in + d_mix * d + pool_w[0].size)
            + n_lvl * TILE * TILE * 4
            + 2 * TILE * d_in * 4 + 2 * TILE * d_mix * 2 + TILE * d * (2 + 4)
            + 4 * TILE * d_in * 4
            + 8 * TILE * TILE * 4)
    const = lambda *shape: _resident(shape, lambda g: (0,) * len(shape))
    at_layer = lambda *shape: _resident((None,) + shape, lambda g: (layer,) + (0,) * len(shape))
    out = pl.pallas_call(
        functools.partial(_mixer_kernel, alpha=alpha, tiles_per_seq=tiles_per_seq),
        grid=(n_tiles + 3,),
        in_specs=[
            pl.BlockSpec((None, TILE, d), lambda g: (jnp.minimum(g, last), 0, 0)),
            pl.BlockSpec((None, TILE, d), lambda g: (jnp.clip(g - 2, 0, last), 0, 0)),
            at_layer(d, d_in),
            at_layer(*pool_w.shape[1:]),
            const(1, pool_scale.shape[-1]),
            const(1, lb.shape[-1]),
            const(1, norm_g.shape[-1]),
            at_layer(d_mix, d),
            const(1, d),
            const(1, d),
            const(*masks.shape),
        ],
        out_specs=pl.BlockSpec((None, TILE, d), lambda g: (jnp.clip(g - 3, 0, last), 0, 0)),
        out_shape=jax.ShapeDtypeStruct((n_tiles, TILE, d), F32),
        scratch_shapes=[
            pltpu.VMEM((d, d_in), BF16), pltpu.VMEM(pool_w.shape[1:], BF16), pltpu.VMEM((d_mix, d), BF16),
            pltpu.VMEM((TILE, d), BF16),
            pltpu.VMEM((2, TILE, d_in), F32),
            pltpu.VMEM((2, TILE, d_mix), BF16),
            pltpu.VMEM((TILE, d), F32),
            pltpu.VMEM((HGRN_HEADS, LANES, LANES), F32),
            pltpu.VMEM((POOL_HALO * SUBLANES, LANES * len(POOL_WINDOWS)), F32),
        ],
        compiler_params=pltpu.CompilerParams(
            dimension_semantics=("arbitrary",),
            vmem_limit_bytes=int(vmem * 1.2)),
        name="mixer",
    )(x_tiles, x_tiles, w_in, pool_w, pool_scale.reshape(1, -1), lb.reshape(1, -1), norm_g.reshape(1, -1),
      w_out, ln_g.reshape(1, d), ln_b.reshape(1, d), masks)
    return out.reshape(bsz, seq, d)


def kernel(x, w_in, pool_w, pool_scale, lb_param, hgrn_norm_g, w_out, ffn1_gate, ffn1_up, ffn1_down,
           ffn2_gate, ffn2_up, ffn2_down, ln_g, ln_b):
    bsz, seq, d = x.shape
    depth = w_in.shape[0]
    alpha = (2.0 * depth) ** 0.25

    lb_all = jnp.cumsum(jax.nn.softmax(lb_param.astype(F32), axis=0), axis=0)
    lb_all = lb_all - lb_all[0:1]

    masks = jnp.asarray(_level_masks())

    x2 = x.reshape(bsz * seq, d)
    for l in range(depth):
        x2 = _ffn(x2, ffn1_gate, ffn1_up, ffn1_down, ln_g[l, 0], ln_b[l, 0], l, alpha=alpha,
                  permute_in=(l == 0))
        xm = _mixer(x2.reshape(bsz, seq, d), w_in, pool_w, pool_scale[l], lb_all[l],
                    hgrn_norm_g[l], w_out, ln_g[l, 1], ln_b[l, 1], masks, l, alpha=alpha)
        x2 = _ffn(xm.reshape(bsz * seq, d), ffn2_gate, ffn2_up, ffn2_down, ln_g[l, 2], ln_b[l, 2], l,
                  alpha=alpha, permute_out=(l == depth - 1))
    return x2.reshape(bsz, seq, d)
```

```python
import functools

import numpy as np
import jax
import jax.numpy as jnp
from jax import lax
from jax.experimental import pallas as pl
from jax.experimental.pallas import tpu as pltpu

F32 = jnp.float32
BF16 = jnp.bfloat16

SUBLANES = 8
LOG2_SUBLANES = SUBLANES.bit_length() - 1
LANES = 128
MXU_DIM = 256

POOL_WINDOWS = (2, 4, 8, 16)
HGRN_HEADS = 4
LN_EPS = 1e-5
RMS_EPS = 1e-6
LOG2E = 1.4426950408889634

TILE = 256
SEG = TILE // SUBLANES
FFN_ROWS = 512
FFN_COLS = MXU_DIM
FFN_LN_SLICES = 8
POOL_HALO = max(POOL_WINDOWS) - 1
MERGE_DEPTHS = 2
MERGE_LOG2_SPAN = 120.0


def _layer_norm(y, g, b):
    mu = jnp.mean(y, axis=-1, keepdims=True)
    yc = y - mu
    var = jnp.mean(yc * yc, axis=-1, keepdims=True)
    return yc * lax.rsqrt(var + LN_EPS) * g + b


def _silu(x):
    return x * jax.nn.sigmoid(x)


def _anchor_zero(*arrays):
    folded = None
    for a in arrays:
        bits = lax.bitcast_convert_type(a, jnp.uint32)
        for r in range(0, a.shape[0], SUBLANES):
            for l in range(0, a.shape[1], LANES):
                piece = bits[r:r + SUBLANES, l:l + LANES]
                folded = piece if folded is None else folded | piece
    return lax.bitcast_convert_type((folded >> 16) >> 16, F32)


def _resident(shape, index_map):
    return pl.BlockSpec(shape, index_map, pipeline_mode=pl.Buffered(1))


def _ffn_kernel(x_ref, wg_hbm, wu_hbm, wd_hbm, g_ref, b_ref, o_ref, wg_ref, wu_ref, wd_ref, stage_g, stage_u,
                stage_d, sems, xb_ref, hid_ref, acc_ref, *permute_refs, alpha, n_tiles, layer, permute_in,
                permute_out):
    i = pl.program_id(0)
    d = acc_ref.shape[2]
    dff = wg_ref.shape[1]
    n_chunks = dff // FFN_COLS
    cur = i % 2
    prev = 1 - cur
    ln_rows = FFN_ROWS // FFN_LN_SLICES
    permute_refs = list(permute_refs)
    xin_ref, sem_x = (permute_refs.pop(0), permute_refs.pop(0)) if permute_in else (None, None)
    out_ref, sem_o = (permute_refs.pop(0), permute_refs.pop(0)) if permute_out else (None, None)

    def segment_copies(tile, slot, to_vmem):
        hbm, vmem, sem = (x_ref, xin_ref, sem_x) if to_vmem else (o_ref, out_ref, sem_o)
        copies = []
        for m in range(FFN_ROWS // TILE):
            for seg in range(SUBLANES):
                there = hbm.at[pl.ds(tile * FFN_ROWS + m * TILE + seg * SEG, SEG), :]
                here = vmem.at[slot, m * SEG:(m + 1) * SEG, seg, :]
                src, dst = (there, here) if to_vmem else (here, there)
                copies.append(pltpu.make_async_copy(src, dst, sem.at[slot, m * SUBLANES + seg]))
        return copies

    def finish_slice(k):
        rows = slice(k * ln_rows, (k + 1) * ln_rows)
        y = _layer_norm(0.5 * acc_ref[prev, rows, :], g_ref[...], b_ref[...])
        if permute_out:
            vregs = ln_rows // SUBLANES
            out_ref[prev, k * vregs:(k + 1) * vregs] = y.reshape(vregs, SUBLANES, d)
        else:
            o_ref[rows, :] = y
        return _anchor_zero(y)

    def weight_copies(c):
        slot = c % 2
        cols = pl.ds(c * FFN_COLS, FFN_COLS)
        return (pltpu.make_async_copy(wg_hbm.at[layer, :, cols], stage_g.at[slot], sems.at[slot, 0]),
                pltpu.make_async_copy(wu_hbm.at[layer, :, cols], stage_u.at[slot], sems.at[slot, 1]),
                pltpu.make_async_copy(wd_hbm.at[layer, cols, :], stage_d.at[slot], sems.at[slot, 2]))

    @pl.when(i == 0)
    def _():
        acc_ref[1] = jnp.zeros(acc_ref.shape[1:], F32)
        if permute_in:
            for cp in segment_copies(0, 0, True):
                cp.start()
        for cp in weight_copies(0):
            cp.start()
        for c in range(n_chunks):
            if c + 1 < n_chunks:
                for cp in weight_copies(c + 1):
                    cp.start()
            for cp in weight_copies(c):
                cp.wait()
            slot = c % 2
            cols = slice(c * FFN_COLS, (c + 1) * FFN_COLS)
            wg_ref[:, cols] = stage_g[slot].astype(BF16)
            wu_ref[:, cols] = stage_u[slot].astype(BF16)
            wd_ref[cols, :] = stage_d[slot].astype(BF16)

    if permute_in:
        @pl.when(i + 1 < n_tiles)
        def _():
            for cp in segment_copies(i + 1, prev, True):
                cp.start()

    if permute_out:
        @pl.when(i >= 3)
        def _():
            for cp in segment_copies(i - 3, prev, False):
                cp.wait()

    @pl.when(i < n_tiles)
    def _():
        if permute_in:
            for cp in segment_copies(i, cur, True):
                cp.wait()
            x = xin_ref[cur].reshape(FFN_ROWS, d)
        else:
            x = x_ref[...]
        xb_ref[...] = x.astype(BF16)
        anchor = None
        for c in range(n_chunks):
            cols = slice(c * FFN_COLS, (c + 1) * FFN_COLS)
            xb = xb_ref[...]
            gate = jnp.dot(xb, wg_ref[:, cols], preferred_element_type=F32)
            up = jnp.dot(xb, wu_ref[:, cols], preferred_element_type=F32)
            if anchor is not None:
                zero = jnp.concatenate([anchor] * (FFN_COLS // LANES), axis=1)
                up = jnp.concatenate([up[:SUBLANES] + zero, up[SUBLANES:]], axis=0)
            hid_ref[:, cols] = (_silu(gate) * up).astype(BF16)
            anchor = finish_slice(c) if c < FFN_LN_SLICES else None
        acc_ref[cur] = (2.0 * alpha) * x + jnp.dot(hid_ref[...], wd_ref[...], preferred_element_type=F32)

    @pl.when(i == n_tiles)
    def _():
        for k in range(FFN_LN_SLICES):
            finish_slice(k)

    if permute_out:
        @pl.when(i >= 1)
        def _():
            for cp in segment_copies(i - 1, prev, False):
                cp.start()

        @pl.when(i == n_tiles)
        def _():
            for tile in (n_tiles - 2, n_tiles - 1):
                for cp in segment_copies(tile, tile % 2, False):
                    cp.wait()


def _ffn(x2d, wg, wu, wd, g, b, layer, *, alpha, permute_in=False, permute_out=False):
    rows, d = x2d.shape
    dff = wg.shape[2]
    assert rows % FFN_ROWS == 0 and dff % FFN_COLS == 0 and FFN_ROWS % TILE == 0
    assert FFN_LN_SLICES <= dff // FFN_COLS and FFN_ROWS % FFN_LN_SLICES == 0
    n_tiles = rows // FFN_ROWS
    assert n_tiles >= 3
    vmem = (3 * d * dff * 2
            + 2 * 3 * d * FFN_COLS * 4
            + 2 * 2 * FFN_ROWS * d * 4
            + FFN_ROWS * d * (2 + 2 * 4)
            + FFN_ROWS * dff * 2
            + 3 * FFN_ROWS * FFN_COLS * 4
            + FFN_ROWS * d * 4)
    in_hbm = pl.BlockSpec(memory_space=pl.ANY)
    permute_scratch = [pltpu.VMEM((2, FFN_ROWS // SUBLANES, SUBLANES, d), F32),
                       pltpu.SemaphoreType.DMA((2, FFN_ROWS // SEG))]
    return pl.pallas_call(
        functools.partial(_ffn_kernel, alpha=alpha, n_tiles=n_tiles, layer=layer, permute_in=permute_in,
                          permute_out=permute_out),
        grid=(n_tiles + 1,),
        in_specs=[
            in_hbm if permute_in else pl.BlockSpec((FFN_ROWS, d), lambda i: (jnp.minimum(i, n_tiles - 1), 0)),
            in_hbm, in_hbm, in_hbm,
            _resident((1, d), lambda i: (0, 0)),
            _resident((1, d), lambda i: (0, 0)),
        ],
        out_specs=in_hbm if permute_out else pl.BlockSpec((FFN_ROWS, d), lambda i: (jnp.maximum(i - 1, 0), 0)),
        out_shape=jax.ShapeDtypeStruct((rows, d), F32),
        scratch_shapes=[
            pltpu.VMEM((d, dff), BF16), pltpu.VMEM((d, dff), BF16), pltpu.VMEM((dff, d), BF16),
            pltpu.VMEM((2, d, FFN_COLS), F32), pltpu.VMEM((2, d, FFN_COLS), F32),
            pltpu.VMEM((2, FFN_COLS, d), F32), pltpu.SemaphoreType.DMA((2, 3)),
            pltpu.VMEM((FFN_ROWS, d), BF16), pltpu.VMEM((FFN_ROWS, dff), BF16),
            pltpu.VMEM((2, FFN_ROWS, d), F32)]
        + (permute_scratch if permute_in else []) + (permute_scratch if permute_out else []),
        compiler_params=pltpu.CompilerParams(
            dimension_semantics=("arbitrary",),
            vmem_limit_bytes=int(vmem * 1.1)),
        name="ffn",
    )(x2d, wg, wu, wd, g.reshape(1, d), b.reshape(1, d))


def _level_plan():
    plan = []
    m = SUBLANES
    while m >= 2:
        plan.append(("seg", m))
        m //= 2
    m = SEG
    while m >= 2:
        plan.append(("pos", m))
        m //= 2
    return plan


def _level_masks():
    r = np.arange(TILE)
    seg, pos = r % SUBLANES, r // SUBLANES
    masks = []
    for kind, m in _level_plan():
        h = m // 2
        if kind == "seg":
            same = (seg[:, None] // m) == (seg[None, :] // m)
            late, early = (seg % m) >= h, (seg % m) < h
        else:
            same = (seg[:, None] == seg[None, :]) & ((pos[:, None] // m) == (pos[None, :] // m))
            late, early = (pos % m) >= h, (pos % m) < h
        masks.append(same & late[:, None] & early[None, :])
    masks = np.stack(masks)
    time = seg * SEG + pos
    causal = time[None, :] < time[:, None]
    assert (masks.sum(0) == causal).all()
    merged = [causal & ((seg[:, None] >> (LOG2_SUBLANES - d)) == (seg[None, :] >> (LOG2_SUBLANES - d)))
              for d in range(MERGE_DEPTHS)]
    for d in range(MERGE_DEPTHS):
        assert (masks[:d].sum(0) + merged[d] == causal).all()
    return np.concatenate([masks, np.stack(merged)]).astype(np.float32)


def _pool_group(ext, u_g, window, tile_in_seq):
    cur = ext[(POOL_HALO - (window - 1)) * SUBLANES:]
    span = 1
    while span < window:
        cur = cur[span * SUBLANES:] + cur[:-span * SUBLANES]
        span *= 2
    row = lax.broadcasted_iota(jnp.int32, (TILE, LANES), 0)
    pos = tile_in_seq * TILE + (row & (SUBLANES - 1)) * SEG + (row >> LOG2_SUBLANES)
    cnt = jnp.minimum(pos + 1, window).astype(F32)
    return cur / cnt - u_g


def _gates(f_raw, lb):
    t = jnp.exp(-jnp.abs(f_raw))
    log_sig = jnp.minimum(f_raw, 0.0) - jnp.log(1.0 + t)
    a = jnp.log(lb)
    c = jnp.log1p(-lb) + log_sig
    log_f = jnp.maximum(a, c) + jnp.log(1.0 + jnp.exp(-jnp.abs(a - c)))
    inv = 1.0 / (1.0 + t)
    sig_neg = jnp.where(f_raw >= 0.0, t * inv, inv)
    return log_f * LOG2E, (1.0 - lb) * sig_neg


def _vregs(a):
    return [a[j * SUBLANES:(j + 1) * SUBLANES] for j in range(a.shape[0] // SUBLANES)]


class _Head:
    def __init__(self, **kw):
        self.__dict__.update(kw)


def _seg_split_offset(off, sub, m):
    ref = off
    for blk in range(SUBLANES // m):
        src = blk * m + m // 2
        in_blk = (sub >= blk * m) & (sub < (blk + 1) * m)
        ref = jnp.where(in_blk, jnp.broadcast_to(off[src:src + 1, :], off.shape), ref)
    return off - ref


def _hgrn_prep(q, f_raw, v, lb, state_t):
    lf, kk = _gates(f_raw, lb)
    qs = q * (LANES ** -0.5)

    lf_v = _vregs(lf)
    b_v = [lf_v[0]]
    for j in range(1, SEG):
        b_v.append(b_v[-1] + lf_v[j])
    b_loc = jnp.concatenate(b_v, axis=0)
    seg_tot = b_v[-1]

    sub = lax.broadcasted_iota(jnp.int32, (SUBLANES, LANES), 0)
    inc = seg_tot
    for sh in (1, 2, 4):
        inc = inc + jnp.where(sub >= sh, pltpu.roll(inc, sh, 0), 0.0)
    off = inc - seg_tot
    total = inc[SUBLANES - 1:SUBLANES, :]
    bg = b_loc + jnp.concatenate([off] * SEG, axis=0)

    q_in = (qs * jnp.exp2(bg)).astype(BF16)
    o = lax.dot_general(q_in, state_t.astype(BF16), (((1,), (1,)), ((), ())),
                        preferred_element_type=F32)
    k_dec = (kk * jnp.exp2(total - bg)).astype(BF16)
    v_b = v.astype(BF16)
    new_state_t = state_t * jnp.exp2(total) + lax.dot_general(
        v_b, k_dec, (((0,), (0,)), ((), ())), preferred_element_type=F32)

    bounds = [off[(SUBLANES >> (d + 1)) * k:(SUBLANES >> (d + 1)) * k + 1, :] if k else jnp.zeros_like(total)
              for d in range(MERGE_DEPTHS) for k in range(2 << d)]
    span = []
    pos = 0
    for d in range(MERGE_DEPTHS):
        edges = bounds[pos:pos + (2 << d)] + [total]
        pos += 2 << d
        worst = edges[0] - edges[1]
        for k in range(1, 2 << d):
            worst = jnp.maximum(worst, edges[k] - edges[k + 1])
        span.append(worst)
    return _Head(qs=qs, kk=kk, v=v, v_b=v_b, b_v=b_v, b_loc=b_loc, off=off, sub=sub, o=o, span=span), new_state_t


def _pair_scores(head, x_q, x_k):
    return lax.dot_general((x_q * head.qs).astype(BF16), (x_k * head.kk).astype(BF16),
                           (((1,), (1,)), ((), ())), preferred_element_type=F32)


def _level_scores(head, lvl, masks_ref):
    kind, m = _level_plan()[lvl]
    h = m // 2
    if kind == "seg":
        row_sub = lax.broadcasted_iota(jnp.int32, (TILE, LANES), 0) & (SUBLANES - 1)
        t = head.b_loc + jnp.concatenate([_seg_split_offset(head.off, head.sub, m)] * SEG, axis=0)
        z = jnp.where((row_sub & (m - 1)) >= h, t, -t)
    else:
        b_v = head.b_v
        z_v = []
        for j in range(SEG):
            ref_v = b_v[(j // m) * m + h - 1]
            z_v.append(b_v[j] - ref_v if j % m >= h else ref_v - b_v[j])
        z = jnp.concatenate(z_v, axis=0)
    x = jnp.exp2(z)
    return masks_ref[lvl] * _pair_scores(head, x, x)


def _merged_scores(head, depth, masks_ref):
    t = head.b_loc + jnp.concatenate(
        [_seg_split_offset(head.off, head.sub, SUBLANES >> depth)] * SEG, axis=0)
    part = _pair_scores(head, jnp.exp2(t), jnp.exp2(-t))
    return jnp.where(masks_ref[len(_level_plan()) + depth] != 0.0, part, 0.0)


def _hgrn_scores(head, variant, masks_ref):
    n_levels = len(_level_plan())
    parts = [_level_scores(head, lvl, masks_ref) for lvl in range(min(variant, n_levels))]
    if variant < MERGE_DEPTHS:
        parts.append(_merged_scores(head, variant, masks_ref))
    scores = parts[0]
    for p in parts[1:]:
        scores = scores + p
    return scores


def _hgrn_finish(head, scores):
    o = head.o + jnp.dot(scores.astype(BF16), head.v_b, preferred_element_type=F32)
    return o + jnp.sum(head.qs * head.kk, axis=-1, keepdims=True) * head.v


def _mixer_kernel(xa_ref, xc_ref, win32_ref, poolw32_ref, pscale_ref, lb_ref, ng_ref, wout32_ref, lng_ref,
                  lnb_ref, masks_ref, o_ref, win_ref, poolw_ref, wout_ref, xab_ref, h_ref, mix_ref, pre_ref,
                  state_ref, tail_ref, *, alpha, tiles_per_seq, n_tiles):
    g = pl.program_id(0)
    last_step = n_tiles + 2

    @pl.when(g == 0)
    def _():
        win_ref[...] = win32_ref[...].astype(BF16)
        poolw_ref[...] = poolw32_ref[...].astype(BF16)
        wout_ref[...] = wout32_ref[...].astype(BF16)
        h_ref[...] = jnp.zeros_like(h_ref)
        mix_ref[...] = jnp.zeros_like(mix_ref)
        pre_ref[...] = jnp.zeros_like(pre_ref)
        state_ref[...] = jnp.zeros_like(state_ref)
        tail_ref[...] = jnp.zeros_like(tail_ref)
        h_ref[0] = jnp.dot(xa_ref[...].astype(BF16), win_ref[...], preferred_element_type=F32)

    @pl.when(g >= last_step - 1)
    def _():
        o_ref[...] = _layer_norm(pre_ref[...], lng_ref[...], lnb_ref[...])

    @pl.when(g == last_step - 1)
    def _():
        pre_ref[...] = alpha * xc_ref[...] + jnp.dot(mix_ref[(last_step - 1) % 2], wout_ref[...],
                                                     preferred_element_type=F32)

    tile_in_seq = (g + tiles_per_seq - 1) % tiles_per_seq
    first = tile_in_seq == 0

    for parity in range(2):
        pl.when((g % 2 == parity) & (g >= 1) & (g < last_step - 1))(functools.partial(
            _mixer_step, parity, 1 - parity, first, tile_in_seq, xa_ref, xc_ref, win_ref, poolw_ref,
            pscale_ref, lb_ref, ng_ref, wout_ref, lng_ref, lnb_ref, masks_ref, o_ref, xab_ref, h_ref,
            mix_ref, pre_ref, state_ref, tail_ref, alpha=alpha))


def _mixer_step(slot_a, slot_b, first, tile_in_seq, xa_ref, xc_ref, win_ref, poolw_ref, pscale_ref, lb_ref,
                ng_ref, wout_ref, lng_ref, lnb_ref, masks_ref, o_ref, xab_ref, h_ref, mix_ref, pre_ref,
                state_ref, tail_ref, *, alpha):
    d_pool = LANES * len(POOL_WINDOWS)
    d_hgrn = LANES * HGRN_HEADS
    halo_rows = POOL_HALO * SUBLANES
    d_out = o_ref.shape[1]
    normed = _layer_norm(pre_ref[...], lng_ref[...], lnb_ref[...])
    o_ref[...] = normed
    ln_anchor = _anchor_zero(normed)
    xab_ref[...] = xa_ref[...].astype(BF16)

    def project(part):
        cols = slice(part * d_hgrn, (part + 1) * d_hgrn)
        h_ref[slot_a, :, cols] = jnp.dot(xab_ref[...], win_ref[:, cols], preferred_element_type=F32)

    def finish(part):
        cols = slice(part * d_out // HGRN_HEADS, (part + 1) * d_out // HGRN_HEADS)
        pre_ref[:, cols] = alpha * xc_ref[:, cols] + jnp.dot(mix_ref[slot_a], wout_ref[:, cols],
                                                            preferred_element_type=F32)

    def hold_projection_until(zero):
        z = jnp.concatenate([zero, zero], axis=0).astype(BF16)
        xab_ref[0:2 * SUBLANES, 0:LANES] = xab_ref[0:2 * SUBLANES, 0:LANES] + z

    project(0)
    hb = h_ref.at[slot_b]

    u = hb[:, :d_pool]
    cur_tail = u[TILE - halo_rows:, :]
    sub = lax.broadcasted_iota(jnp.int32, cur_tail.shape, 0) & (SUBLANES - 1)
    prev_tail = jnp.where(first, 0.0, tail_ref[...])
    head = jnp.where(sub == 0,
                     pltpu.roll(prev_tail, halo_rows - (SUBLANES - 1), 0),
                     pltpu.roll(cur_tail, 1, 0))
    tail_ref[...] = cur_tail
    ext = jnp.concatenate([head, u], axis=0)
    pooled_out = []
    for grp, window in enumerate(POOL_WINDOWS):
        cols = slice(grp * LANES, (grp + 1) * LANES)
        pooled = _pool_group(ext[:, cols], u[:, cols], window, tile_in_seq)
        y = jnp.dot(pooled.astype(BF16), poolw_ref[grp], preferred_element_type=F32) * pscale_ref[:, cols]
        mix_ref[slot_b, :, cols] = y.astype(BF16)
        pooled_out.append(y)
    anchors = [ln_anchor, _anchor_zero(*pooled_out)]

    def part(hd, k):
        lo = d_pool + k * d_hgrn + hd * LANES
        return hb[:, lo:lo + LANES]

    heads = []
    for hd in range(HGRN_HEADS):
        if hd > 0:
            hold_projection_until(anchors[hd])
        project(hd + 1)
        state = jnp.where(first, 0.0, state_ref[hd])
        head, new_state = _hgrn_prep(part(hd, 0), part(hd, 1), part(hd, 2),
                                     lb_ref[:, hd * LANES:(hd + 1) * LANES], state)
        state_ref[hd] = new_state
        heads.append(head)
        anchors.append(_anchor_zero(head.kk, head.b_loc))

    worst = [jnp.max(functools.reduce(jnp.maximum, [hd.span[d] for hd in heads])) for d in range(MERGE_DEPTHS)]

    def mix_heads(variant):
        for hd, head in enumerate(heads):
            o = _hgrn_finish(head, _hgrn_scores(head, variant, masks_ref))
            o = o * lax.rsqrt(jnp.mean(o * o, axis=-1, keepdims=True) + RMS_EPS) * ng_ref[...]
            mix_ref[slot_b, :, d_pool + hd * LANES:d_pool + (hd + 1) * LANES] = (
                o * _silu(part(hd, 3))).astype(BF16)
            finish(hd)

    ok0 = worst[0] <= MERGE_LOG2_SPAN
    ok1 = worst[1] <= MERGE_LOG2_SPAN
    pl.when(ok0)(functools.partial(mix_heads, 0))
    pl.when(jnp.logical_not(ok0) & ok1)(functools.partial(mix_heads, 1))
    pl.when(jnp.logical_not(ok0 | ok1))(functools.partial(mix_heads, len(_level_plan())))


def _mixer(x3d, w_in, pool_w, pool_scale, lb, norm_g, w_out, ln_g, ln_b, masks, layer, *, alpha):
    bsz, seq, d = x3d.shape
    d_in = w_in.shape[2]
    d_mix = w_out.shape[1]
    assert seq % TILE == 0
    tiles_per_seq = seq // TILE
    n_tiles = bsz * tiles_per_seq
    n_lvl = masks.shape[0]
    x_tiles = x3d.reshape(n_tiles, TILE, d)
    last = n_tiles - 1
    vmem = (3 * 2 * TILE * d * 4
            + (4 + 2) * (d * d_in + d_mix * d + pool_w[0].size)
            + n_lvl * TILE * TILE * 4
            + 2 * TILE * d_in * 4 + 2 * TILE * d_mix * 2 + TILE * d * (2 + 4)
            + 4 * TILE * d_in * 4
            + 8 * TILE * TILE * 4)
    const = lambda *shape: _resident(shape, lambda g: (0,) * len(shape))
    at_layer = lambda *shape: _resident((None,) + shape, lambda g: (layer,) + (0,) * len(shape))
    out = pl.pallas_call(
        functools.partial(_mixer_kernel, alpha=alpha, tiles_per_seq=tiles_per_seq, n_tiles=n_tiles),
        grid=(n_tiles + 3,),
        in_specs=[
            pl.BlockSpec((None, TILE, d), lambda g: (jnp.minimum(g, last), 0, 0)),
            pl.BlockSpec((None, TILE, d), lambda g: (jnp.clip(g - 2, 0, last), 0, 0)),
            at_layer(d, d_in),
            at_layer(*pool_w.shape[1:]),
            const(1, pool_scale.shape[-1]),
            const(1, lb.shape[-1]),
            const(1, norm_g.shape[-1]),
            at_layer(d_mix, d),
            const(1, d),
            const(1, d),
            const(*masks.shape),
        ],
        out_specs=pl.BlockSpec((None, TILE, d), lambda g: (jnp.clip(g - 3, 0, last), 0, 0)),
        out_shape=jax.ShapeDtypeStruct((n_tiles, TILE, d), F32),
        scratch_shapes=[
            pltpu.VMEM((d, d_in), BF16), pltpu.VMEM(pool_w.shape[1:], BF16), pltpu.VMEM((d_mix, d), BF16),
            pltpu.VMEM((TILE, d), BF16),
            pltpu.VMEM((2, TILE, d_in), F32),
            pltpu.VMEM((2, TILE, d_mix), BF16),
            pltpu.VMEM((TILE, d), F32),
            pltpu.VMEM((HGRN_HEADS, LANES, LANES), F32),
            pltpu.VMEM((POOL_HALO * SUBLANES, LANES * len(POOL_WINDOWS)), F32),
        ],
        compiler_params=pltpu.CompilerParams(
            dimension_semantics=("arbitrary",),
            vmem_limit_bytes=int(vmem * 1.2)),
        name="mixer",
    )(x_tiles, x_tiles, w_in, pool_w, pool_scale.reshape(1, -1), lb.reshape(1, -1), norm_g.reshape(1, -1),
      w_out, ln_g.reshape(1, d), ln_b.reshape(1, d), masks)
    return out.reshape(bsz, seq, d)


def kernel(x, w_in, pool_w, pool_scale, lb_param, hgrn_norm_g, w_out, ffn1_gate, ffn1_up, ffn1_down,
           ffn2_gate, ffn2_up, ffn2_down, ln_g, ln_b):
    bsz, seq, d = x.shape
    depth = w_in.shape[0]
    alpha = (2.0 * depth) ** 0.25

    lb_all = jnp.cumsum(jax.nn.softmax(lb_param.astype(F32), axis=0), axis=0)
    lb_all = lb_all - lb_all[0:1]

    masks = jnp.asarray(_level_masks())

    x2 = x.reshape(bsz * seq, d)
    for l in range(depth):
        x2 = _ffn(x2, ffn1_gate, ffn1_up, ffn1_down, ln_g[l, 0], ln_b[l, 0], l, alpha=alpha,
                  permute_in=(l == 0))
        xm = _mixer(x2.reshape(bsz, seq, d), w_in, pool_w, pool_scale[l], lb_all[l],
                    hgrn_norm_g[l], w_out, ln_g[l, 1], ln_b[l, 1], masks, l, alpha=alpha)
        x2 = _ffn(xm.reshape(bsz * seq, d), ffn2_gate, ffn2_up, ffn2_down, ln_g[l, 2], ln_b[l, 2], l,
                  alpha=alpha, permute_out=(l == depth - 1))
    return x2.reshape(bsz, seq, d)
```

```python
import functools

import numpy as np
import jax
import jax.numpy as jnp
from jax import lax
from jax.experimental import pallas as pl
from jax.experimental.pallas import tpu as pltpu

F32 = jnp.float32
BF16 = jnp.bfloat16

SUBLANES = 8
LOG2_SUBLANES = SUBLANES.bit_length() - 1
LANES = 128
MXU_DIM = 256

POOL_WINDOWS = (2, 4, 8, 16)
HGRN_HEADS = 4
LN_EPS = 1e-5
RMS_EPS = 1e-6
LOG2E = 1.4426950408889634

TILE = 256
SEG = TILE // SUBLANES
FFN_ROWS = 512
FFN_COLS = MXU_DIM
FFN_LN_SLICES = 8
POOL_HALO = max(POOL_WINDOWS) - 1
MERGE_DEPTHS = 2
MERGE_LOG2_SPAN = 120.0


def _layer_norm(y, g, b):
    mu = jnp.mean(y, axis=-1, keepdims=True)
    yc = y - mu
    var = jnp.mean(yc * yc, axis=-1, keepdims=True)
    return yc * lax.rsqrt(var + LN_EPS) * g + b


def _silu(x):
    return x * jax.nn.sigmoid(x)


def _anchor_zero(*arrays):
    folded = None
    for a in arrays:
        bits = lax.bitcast_convert_type(a, jnp.uint32)
        for r in range(0, a.shape[0], SUBLANES):
            for l in range(0, a.shape[1], LANES):
                piece = bits[r:r + SUBLANES, l:l + LANES]
                folded = piece if folded is None else folded | piece
    return lax.bitcast_convert_type((folded >> 16) >> 16, F32)


def _resident(shape, index_map):
    return pl.BlockSpec(shape, index_map, pipeline_mode=pl.Buffered(1))


def _ffn_kernel(x_ref, wg_hbm, wu_hbm, wd_hbm, g_ref, b_ref, o_ref, wg_ref, wu_ref, wd_ref, stage_g, stage_u,
                stage_d, sems, xb_ref, hid_ref, acc_ref, *permute_refs, alpha, n_tiles, layer, permute_in,
                permute_out):
    i = pl.program_id(0)
    d = acc_ref.shape[2]
    dff = wg_ref.shape[1]
    n_chunks = dff // FFN_COLS
    cur = i % 2
    prev = 1 - cur
    ln_rows = FFN_ROWS // FFN_LN_SLICES
    permute_refs = list(permute_refs)
    xin_ref, sem_x = (permute_refs.pop(0), permute_refs.pop(0)) if permute_in else (None, None)
    out_ref, sem_o = (permute_refs.pop(0), permute_refs.pop(0)) if permute_out else (None, None)

    def segment_copies(tile, slot, to_vmem):
        hbm, vmem, sem = (x_ref, xin_ref, sem_x) if to_vmem else (o_ref, out_ref, sem_o)
        copies = []
        for m in range(FFN_ROWS // TILE):
            for seg in range(SUBLANES):
                there = hbm.at[pl.ds(tile * FFN_ROWS + m * TILE + seg * SEG, SEG), :]
                here = vmem.at[slot, m * SEG:(m + 1) * SEG, seg, :]
                src, dst = (there, here) if to_vmem else (here, there)
                copies.append(pltpu.make_async_copy(src, dst, sem.at[slot, m * SUBLANES + seg]))
        return copies

    def finish_slice(k):
        rows = slice(k * ln_rows, (k + 1) * ln_rows)
        y = _layer_norm(0.5 * acc_ref[prev, rows, :], g_ref[...], b_ref[...])
        if permute_out:
            vregs = ln_rows // SUBLANES
            out_ref[prev, k * vregs:(k + 1) * vregs] = y.reshape(vregs, SUBLANES, d)
        else:
            o_ref[rows, :] = y
        return _anchor_zero(y)

    def weight_copies(c):
        slot = c % 2
        cols = pl.ds(c * FFN_COLS, FFN_COLS)
        return (pltpu.make_async_copy(wg_hbm.at[layer, :, cols], stage_g.at[slot], sems.at[slot, 0]),
                pltpu.make_async_copy(wu_hbm.at[layer, :, cols], stage_u.at[slot], sems.at[slot, 1]),
                pltpu.make_async_copy(wd_hbm.at[layer, cols, :], stage_d.at[slot], sems.at[slot, 2]))

    @pl.when(i == 0)
    def _():
        acc_ref[1] = jnp.zeros(acc_ref.shape[1:], F32)
        if permute_in:
            for cp in segment_copies(0, 0, True):
                cp.start()
        for cp in weight_copies(0):
            cp.start()
        for c in range(n_chunks):
            if c + 1 < n_chunks:
                for cp in weight_copies(c + 1):
                    cp.start()
            for cp in weight_copies(c):
                cp.wait()
            slot = c % 2
            cols = slice(c * FFN_COLS, (c + 1) * FFN_COLS)
            wg_ref[:, cols] = stage_g[slot].astype(BF16)
            wu_ref[:, cols] = stage_u[slot].astype(BF16)
            wd_ref[cols, :] = stage_d[slot].astype(BF16)

    if permute_in:
        @pl.when(i + 1 < n_tiles)
        def _():
            for cp in segment_copies(i + 1, prev, True):
                cp.start()

    if permute_out:
        @pl.when(i >= 3)
        def _():
            for cp in segment_copies(i - 3, prev, False):
                cp.wait()

    @pl.when(i < n_tiles)
    def _():
        if permute_in:
            for cp in segment_copies(i, cur, True):
                cp.wait()
            x = xin_ref[cur].reshape(FFN_ROWS, d)
        else:
            x = x_ref[...]
        xb_ref[...] = x.astype(BF16)
        anchor = None
        for c in range(n_chunks):
            cols = slice(c * FFN_COLS, (c + 1) * FFN_COLS)
            xb = xb_ref[...]
            gate = jnp.dot(xb, wg_ref[:, cols], preferred_element_type=F32)
            up = jnp.dot(xb, wu_ref[:, cols], preferred_element_type=F32)
            if anchor is not None:
                zero = jnp.concatenate([anchor] * (FFN_COLS // LANES), axis=1)
                up = jnp.concatenate([up[:SUBLANES] + zero, up[SUBLANES:]], axis=0)
            hid_ref[:, cols] = (_silu(gate) * up).astype(BF16)
            anchor = finish_slice(c) if c < FFN_LN_SLICES else None
        acc_ref[cur] = (2.0 * alpha) * x + jnp.dot(hid_ref[...], wd_ref[...], preferred_element_type=F32)

    @pl.when(i == n_tiles)
    def _():
        for k in range(FFN_LN_SLICES):
            finish_slice(k)

    if permute_out:
        @pl.when(i >= 1)
        def _():
            for cp in segment_copies(i - 1, prev, False):
                cp.start()

        @pl.when(i == n_tiles)
        def _():
            for tile in (n_tiles - 2, n_tiles - 1):
                for cp in segment_copies(tile, tile % 2, False):
                    cp.wait()


def _ffn(x2d, wg, wu, wd, g, b, layer, *, alpha, permute_in=False, permute_out=False):
    rows, d = x2d.shape
    dff = wg.shape[2]
    assert rows % FFN_ROWS == 0 and dff % FFN_COLS == 0 and FFN_ROWS % TILE == 0
    assert FFN_LN_SLICES <= dff // FFN_COLS and FFN_ROWS % FFN_LN_SLICES == 0
    n_tiles = rows // FFN_ROWS
    assert n_tiles >= 3
    vmem = (3 * d * dff * 2
            + 2 * 3 * d * FFN_COLS * 4
            + 2 * 2 * FFN_ROWS * d * 4
            + FFN_ROWS * d * (2 + 2 * 4)
            + FFN_ROWS * dff * 2
            + 3 * FFN_ROWS * FFN_COLS * 4
            + FFN_ROWS * d * 4)
    in_hbm = pl.BlockSpec(memory_space=pl.ANY)
    permute_scratch = [pltpu.VMEM((2, FFN_ROWS // SUBLANES, SUBLANES, d), F32),
                       pltpu.SemaphoreType.DMA((2, FFN_ROWS // SEG))]
    return pl.pallas_call(
        functools.partial(_ffn_kernel, alpha=alpha, n_tiles=n_tiles, layer=layer, permute_in=permute_in,
                          permute_out=permute_out),
        grid=(n_tiles + 1,),
        in_specs=[
            in_hbm if permute_in else pl.BlockSpec((FFN_ROWS, d), lambda i: (jnp.minimum(i, n_tiles - 1), 0)),
            in_hbm, in_hbm, in_hbm,
            _resident((1, d), lambda i: (0, 0)),
            _resident((1, d), lambda i: (0, 0)),
        ],
        out_specs=in_hbm if permute_out else pl.BlockSpec((FFN_ROWS, d), lambda i: (jnp.maximum(i - 1, 0), 0)),
        out_shape=jax.ShapeDtypeStruct((rows, d), F32),
        scratch_shapes=[
            pltpu.VMEM((d, dff), BF16), pltpu.VMEM((d, dff), BF16), pltpu.VMEM((dff, d), BF16),
            pltpu.VMEM((2, d, FFN_COLS), F32), pltpu.VMEM((2, d, FFN_COLS), F32),
            pltpu.VMEM((2, FFN_COLS, d), F32), pltpu.SemaphoreType.DMA((2, 3)),
            pltpu.VMEM((FFN_ROWS, d), BF16), pltpu.VMEM((FFN_ROWS, dff), BF16),
            pltpu.VMEM((2, FFN_ROWS, d), F32)]
        + (permute_scratch if permute_in else []) + (permute_scratch if permute_out else []),
        compiler_params=pltpu.CompilerParams(
            dimension_semantics=("arbitrary",),
            vmem_limit_bytes=int(vmem * 1.1)),
        name="ffn",
    )(x2d, wg, wu, wd, g.reshape(1, d), b.reshape(1, d))


def _level_plan():
    plan = []
    m = SUBLANES
    while m >= 2:
        plan.append(("seg", m))
        m //= 2
    m = SEG
    while m >= 2:
        plan.append(("pos", m))
        m //= 2
    return plan


def _level_masks():
    r = np.arange(TILE)
    seg, pos = r % SUBLANES, r // SUBLANES
    masks = []
    for kind, m in _level_plan():
        h = m // 2
        if kind == "seg":
            same = (seg[:, None] // m) == (seg[None, :] // m)
            late, early = (seg % m) >= h, (seg % m) < h
        else:
            same = (seg[:, None] == seg[None, :]) & ((pos[:, None] // m) == (pos[None, :] // m))
            late, early = (pos % m) >= h, (pos % m) < h
        masks.append(same & late[:, None] & early[None, :])
    masks = np.stack(masks)
    time = seg * SEG + pos
    causal = time[None, :] < time[:, None]
    assert (masks.sum(0) == causal).all()
    merged = [causal & ((seg[:, None] >> (LOG2_SUBLANES - d)) == (seg[None, :] >> (LOG2_SUBLANES - d)))
              for d in range(MERGE_DEPTHS)]
    for d in range(MERGE_DEPTHS):
        assert (masks[:d].sum(0) + merged[d] == causal).all()
    return np.concatenate([masks, np.stack(merged)]).astype(np.float32)


def _pool_group(ext, u_g, window, tile_in_seq):
    cur = ext[(POOL_HALO - (window - 1)) * SUBLANES:]
    span = 1
    while span < window:
        cur = cur[span * SUBLANES:] + cur[:-span * SUBLANES]
        span *= 2
    row = lax.broadcasted_iota(jnp.int32, (TILE, LANES), 0)
    pos = tile_in_seq * TILE + (row & (SUBLANES - 1)) * SEG + (row >> LOG2_SUBLANES)
    cnt = jnp.minimum(pos + 1, window).astype(F32)
    return cur / cnt - u_g


def _gates(f_raw, lb):
    t = jnp.exp(-jnp.abs(f_raw))
    log_sig = jnp.minimum(f_raw, 0.0) - jnp.log(1.0 + t)
    a = jnp.log(lb)
    c = jnp.log1p(-lb) + log_sig
    log_f = jnp.maximum(a, c) + jnp.log(1.0 + jnp.exp(-jnp.abs(a - c)))
    inv = 1.0 / (1.0 + t)
    sig_neg = jnp.where(f_raw >= 0.0, t * inv, inv)
    return log_f * LOG2E, (1.0 - lb) * sig_neg


def _vregs(a):
    return [a[j * SUBLANES:(j + 1) * SUBLANES] for j in range(a.shape[0] // SUBLANES)]


class _Head:
    def __init__(self, **kw):
        self.__dict__.update(kw)


def _seg_split_offset(off, sub, m):
    ref = off
    for blk in range(SUBLANES // m):
        src = blk * m + m // 2
        in_blk = (sub >= blk * m) & (sub < (blk + 1) * m)
        ref = jnp.where(in_blk, jnp.broadcast_to(off[src:src + 1, :], off.shape), ref)
    return off - ref


def _hgrn_prep(q, f_raw, v, lb, state_t):
    lf, kk = _gates(f_raw, lb)
    qs = q * (LANES ** -0.5)

    lf_v = _vregs(lf)
    b_v = [lf_v[0]]
    for j in range(1, SEG):
        b_v.append(b_v[-1] + lf_v[j])
    b_loc = jnp.concatenate(b_v, axis=0)
    seg_tot = b_v[-1]

    sub = lax.broadcasted_iota(jnp.int32, (SUBLANES, LANES), 0)
    inc = seg_tot
    for sh in (1, 2, 4):
        inc = inc + jnp.where(sub >= sh, pltpu.roll(inc, sh, 0), 0.0)
    off = inc - seg_tot
    total = inc[SUBLANES - 1:SUBLANES, :]
    bg = b_loc + jnp.concatenate([off] * SEG, axis=0)

    q_in = (qs * jnp.exp2(bg)).astype(BF16)
    o = lax.dot_general(q_in, state_t.astype(BF16), (((1,), (1,)), ((), ())),
                        preferred_element_type=F32)
    k_dec = (kk * jnp.exp2(total - bg)).astype(BF16)
    v_b = v.astype(BF16)
    new_state_t = state_t * jnp.exp2(total) + lax.dot_general(
        v_b, k_dec, (((0,), (0,)), ((), ())), preferred_element_type=F32)

    bounds = [off[(SUBLANES >> (d + 1)) * k:(SUBLANES >> (d + 1)) * k + 1, :] if k else jnp.zeros_like(total)
              for d in range(MERGE_DEPTHS) for k in range(2 << d)]
    span = []
    pos = 0
    for d in range(MERGE_DEPTHS):
        edges = bounds[pos:pos + (2 << d)] + [total]
        pos += 2 << d
        worst = edges[0] - edges[1]
        for k in range(1, 2 << d):
            worst = jnp.maximum(worst, edges[k] - edges[k + 1])
        span.append(worst)
    return _Head(qs=qs, kk=kk, v=v, v_b=v_b, b_v=b_v, b_loc=b_loc, off=off, sub=sub, o=o, span=span), new_state_t


def _pair_scores(head, x_q, x_k):
    return lax.dot_general((x_q * head.qs).astype(BF16), (x_k * head.kk).astype(BF16),
                           (((1,), (1,)), ((), ())), preferred_element_type=F32)


def _level_scores(head, lvl, masks_ref):
    kind, m = _level_plan()[lvl]
    h = m // 2
    if kind == "seg":
        row_sub = lax.broadcasted_iota(jnp.int32, (TILE, LANES), 0) & (SUBLANES - 1)
        t = head.b_loc + jnp.concatenate([_seg_split_offset(head.off, head.sub, m)] * SEG, axis=0)
        z = jnp.where((row_sub & (m - 1)) >= h, t, -t)
    else:
        b_v = head.b_v
        z_v = []
        for j in range(SEG):
            ref_v = b_v[(j // m) * m + h - 1]
            z_v.append(b_v[j] - ref_v if j % m >= h else ref_v - b_v[j])
        z = jnp.concatenate(z_v, axis=0)
    x = jnp.exp2(z)
    return masks_ref[lvl] * _pair_scores(head, x, x)


def _merged_scores(head, depth, masks_ref):
    t = head.b_loc + jnp.concatenate(
        [_seg_split_offset(head.off, head.sub, SUBLANES >> depth)] * SEG, axis=0)
    part = _pair_scores(head, jnp.exp2(t), jnp.exp2(-t))
    return jnp.where(masks_ref[len(_level_plan()) + depth] != 0.0, part, 0.0)


def _hgrn_scores(head, variant, masks_ref):
    n_levels = len(_level_plan())
    parts = [_level_scores(head, lvl, masks_ref) for lvl in range(min(variant, n_levels))]
    if variant < MERGE_DEPTHS:
        parts.append(_merged_scores(head, variant, masks_ref))
    scores = parts[0]
    for p in parts[1:]:
        scores = scores + p
    return scores


def _hgrn_finish(head, scores):
    o = head.o + jnp.dot(scores.astype(BF16), head.v_b, preferred_element_type=F32)
    return o + jnp.sum(head.qs * head.kk, axis=-1, keepdims=True) * head.v


def _mixer_kernel(xa_ref, xc_ref, win32_ref, poolw32_ref, pscale_ref, lb_ref, ng_ref, wout32_ref, lng_ref,
                  lnb_ref, masks_ref, o_ref, win_ref, poolw_ref, wout_ref, xab_ref, h_ref, mix_ref, pre_ref,
                  state_ref, tail_ref, *, alpha, tiles_per_seq, n_tiles):
    g = pl.program_id(0)
    last_step = n_tiles + 2

    @pl.when(g == 0)
    def _():
        win_ref[...] = win32_ref[...].astype(BF16)
        poolw_ref[...] = poolw32_ref[...].astype(BF16)
        wout_ref[...] = wout32_ref[...].astype(BF16)
        h_ref[...] = jnp.zeros_like(h_ref)
        mix_ref[...] = jnp.zeros_like(mix_ref)
        pre_ref[...] = jnp.zeros_like(pre_ref)
        state_ref[...] = jnp.zeros_like(state_ref)
        tail_ref[...] = jnp.zeros_like(tail_ref)
        h_ref[0] = jnp.dot(xa_ref[...].astype(BF16), win_ref[...], preferred_element_type=F32)

    @pl.when(g >= last_step - 1)
    def _():
        o_ref[...] = _layer_norm(pre_ref[...], lng_ref[...], lnb_ref[...])

    @pl.when(g == last_step - 1)
    def _():
        pre_ref[...] = alpha * xc_ref[...] + jnp.dot(mix_ref[(last_step - 1) % 2], wout_ref[...],
                                                     preferred_element_type=F32)

    tile_in_seq = (g + tiles_per_seq - 1) % tiles_per_seq
    first = tile_in_seq == 0

    for parity in range(2):
        pl.when((g % 2 == parity) & (g >= 1) & (g < last_step - 1))(functools.partial(
            _mixer_step, parity, 1 - parity, first, tile_in_seq, xa_ref, xc_ref, win_ref, poolw_ref,
            pscale_ref, lb_ref, ng_ref, wout_ref, lng_ref, lnb_ref, masks_ref, o_ref, xab_ref, h_ref,
            mix_ref, pre_ref, state_ref, tail_ref, alpha=alpha))


def _mixer_step(slot_a, slot_b, first, tile_in_seq, xa_ref, xc_ref, win_ref, poolw_ref, pscale_ref, lb_ref,
                ng_ref, wout_ref, lng_ref, lnb_ref, masks_ref, o_ref, xab_ref, h_ref, mix_ref, pre_ref,
                state_ref, tail_ref, *, alpha):
    d_pool = LANES * len(POOL_WINDOWS)
    d_hgrn = LANES * HGRN_HEADS
    halo_rows = POOL_HALO * SUBLANES
    d_out = o_ref.shape[1]
    o_ref[...] = _layer_norm(pre_ref[...], lng_ref[...], lnb_ref[...])
    xab_ref[...] = xa_ref[...].astype(BF16)

    def project(part):
        cols = slice(part * d_hgrn, (part + 1) * d_hgrn)
        h_ref[slot_a, :, cols] = jnp.dot(xab_ref[...], win_ref[:, cols], preferred_element_type=F32)

    def finish(part):
        cols = slice(part * d_out // HGRN_HEADS, (part + 1) * d_out // HGRN_HEADS)
        pre_ref[:, cols] = alpha * xc_ref[:, cols] + jnp.dot(mix_ref[slot_a], wout_ref[:, cols],
                                                            preferred_element_type=F32)

    project(0)
    hb = h_ref.at[slot_b]

    u = hb[:, :d_pool]
    cur_tail = u[TILE - halo_rows:, :]
    sub = lax.broadcasted_iota(jnp.int32, cur_tail.shape, 0) & (SUBLANES - 1)
    prev_tail = jnp.where(first, 0.0, tail_ref[...])
    head = jnp.where(sub == 0,
                     pltpu.roll(prev_tail, halo_rows - (SUBLANES - 1), 0),
                     pltpu.roll(cur_tail, 1, 0))
    tail_ref[...] = cur_tail
    ext = jnp.concatenate([head, u], axis=0)
    for grp, window in enumerate(POOL_WINDOWS):
        cols = slice(grp * LANES, (grp + 1) * LANES)
        pooled = _pool_group(ext[:, cols], u[:, cols], window, tile_in_seq)
        y = jnp.dot(pooled.astype(BF16), poolw_ref[grp], preferred_element_type=F32) * pscale_ref[:, cols]
        mix_ref[slot_b, :, cols] = y.astype(BF16)

    def part(hd, k):
        lo = d_pool + k * d_hgrn + hd * LANES
        return hb[:, lo:lo + LANES]

    heads = []
    for hd in range(HGRN_HEADS):
        project(hd + 1)
        state = jnp.where(first, 0.0, state_ref[hd])
        head, new_state = _hgrn_prep(part(hd, 0), part(hd, 1), part(hd, 2),
                                     lb_ref[:, hd * LANES:(hd + 1) * LANES], state)
        state_ref[hd] = new_state
        heads.append(head)

    worst = [jnp.max(functools.reduce(jnp.maximum, [hd.span[d] for hd in heads])) for d in range(MERGE_DEPTHS)]

    def mix_heads(variant):
        for hd, head in enumerate(heads):
            o = _hgrn_finish(head, _hgrn_scores(head, variant, masks_ref))
            o = o * lax.rsqrt(jnp.mean(o * o, axis=-1, keepdims=True) + RMS_EPS) * ng_ref[...]
            mix_ref[slot_b, :, d_pool + hd * LANES:d_pool + (hd + 1) * LANES] = (
                o * _silu(part(hd, 3))).astype(BF16)
            finish(hd)

    ok0 = worst[0] <= MERGE_LOG2_SPAN
    ok1 = worst[1] <= MERGE_LOG2_SPAN
    pl.when(ok0)(functools.partial(mix_heads, 0))
    pl.when(jnp.logical_not(ok0) & ok1)(functools.partial(mix_heads, 1))
    pl.when(jnp.logical_not(ok0 | ok1))(functools.partial(mix_heads, len(_level_plan())))


def _mixer(x3d, w_in, pool_w, pool_scale, lb, norm_g, w_out, ln_g, ln_b, masks, layer, *, alpha):
    bsz, seq, d = x3d.shape
    d_in = w_in.shape[2]
    d_mix = w_out.shape[1]
    assert seq % TILE == 0
    tiles_per_seq = seq // TILE
    n_tiles = bsz * tiles_per_seq
    n_lvl = masks.shape[0]
    x_tiles = x3d.reshape(n_tiles, TILE, d)
    last = n_tiles - 1
    vmem = (3 * 2 * TILE * d * 4
            + (4 + 2) * (d * d_in + d_mix * d + pool_w[0].size)
            + n_lvl * TILE * TILE * 4
            + 2 * TILE * d_in * 4 + 2 * TILE * d_mix * 2 + TILE * d * (2 + 4)
            + 4 * TILE * d_in * 4
            + 8 * TILE * TILE * 4)
    const = lambda *shape: _resident(shape, lambda g: (0,) * len(shape))
    at_layer = lambda *shape: _resident((None,) + shape, lambda g: (layer,) + (0,) * len(shape))
    out = pl.pallas_call(
        functools.partial(_mixer_kernel, alpha=alpha, tiles_per_seq=tiles_per_seq, n_tiles=n_tiles),
        grid=(n_tiles + 3,),
        in_specs=[
            pl.BlockSpec((None, TILE, d), lambda g: (jnp.minimum(g, last), 0, 0)),
            pl.BlockSpec((None, TILE, d), lambda g: (jnp.clip(g - 2, 0, last), 0, 0)),
            at_layer(d, d_in),
            at_layer(*pool_w.shape[1:]),
            const(1, pool_scale.shape[-1]),
            const(1, lb.shape[-1]),
            const(1, norm_g.shape[-1]),
            at_layer(d_mix, d),
            const(1, d),
            const(1, d),
            const(*masks.shape),
        ],
        out_specs=pl.BlockSpec((None, TILE, d), lambda g: (jnp.clip(g - 3, 0, last), 0, 0)),
        out_shape=jax.ShapeDtypeStruct((n_tiles, TILE, d), F32),
        scratch_shapes=[
            pltpu.VMEM((d, d_in), BF16), pltpu.VMEM(pool_w.shape[1:], BF16), pltpu.VMEM((d_mix, d), BF16),
            pltpu.VMEM((TILE, d), BF16),
            pltpu.VMEM((2, TILE, d_in), F32),
            pltpu.VMEM((2, TILE, d_mix), BF16),
            pltpu.VMEM((TILE, d), F32),
            pltpu.VMEM((HGRN_HEADS, LANES, LANES), F32),
            pltpu.VMEM((POOL_HALO * SUBLANES, LANES * len(POOL_WINDOWS)), F32),
        ],
        compiler_params=pltpu.CompilerParams(
            dimension_semantics=("arbitrary",),
            vmem_limit_bytes=int(vmem * 1.2)),
        name="mixer",
    )(x_tiles, x_tiles, w_in, pool_w, pool_scale.reshape(1, -1), lb.reshape(1, -1), norm_g.reshape(1, -1),
      w_out, ln_g.reshape(1, d), ln_b.reshape(1, d), masks)
    return out.reshape(bsz, seq, d)


def kernel(x, w_in, pool_w, pool_scale, lb_param, hgrn_norm_g, w_out, ffn1_gate, ffn1_up, ffn1_down,
           ffn2_gate, ffn2_up, ffn2_down, ln_g, ln_b):
    bsz, seq, d = x.shape
    depth = w_in.shape[0]
    alpha = (2.0 * depth) ** 0.25

    lb_all = jnp.cumsum(jax.nn.softmax(lb_param.astype(F32), axis=0), axis=0)
    lb_all = lb_all - lb_all[0:1]

    masks = jnp.asarray(_level_masks())

    x2 = x.reshape(bsz * seq, d)
    for l in range(depth):
        x2 = _ffn(x2, ffn1_gate, ffn1_up, ffn1_down, ln_g[l, 0], ln_b[l, 0], l, alpha=alpha,
                  permute_in=(l == 0))
        xm = _mixer(x2.reshape(bsz, seq, d), w_in, pool_w, pool_scale[l], lb_all[l],
                    hgrn_norm_g[l], w_out, ln_g[l, 1], ln_b[l, 1], masks, l, alpha=alpha)
        x2 = _ffn(xm.reshape(bsz * seq, d), ffn2_gate, ffn2_up, ffn2_down, ln_g[l, 2], ln_b[l, 2], l,
                  alpha=alpha, permute_out=(l == depth - 1))
    return x2.reshape(bsz, seq, d)
```

```python
import functools

import numpy as np
import jax
import jax.numpy as jnp
from jax import lax
from jax.experimental import pallas as pl
from jax.experimental.pallas import tpu as pltpu

F32 = jnp.float32
BF16 = jnp.bfloat16

SUBLANES = 8
LOG2_SUBLANES = SUBLANES.bit_length() - 1
LANES = 128
MXU_DIM = 256

POOL_WINDOWS = (2, 4, 8, 16)
HGRN_HEADS = 4
LN_EPS = 1e-5
RMS_EPS = 1e-6
LOG2E = 1.4426950408889634

TILE = 256
SEG = TILE // SUBLANES
FFN_ROWS = 512
FFN_COLS = MXU_DIM
FFN_LN_SLICES = 8
POOL_HALO = max(POOL_WINDOWS) - 1
MERGE_DEPTHS = 2
MERGE_LOG2_SPAN = 120.0


def _layer_norm(y, g, b):
    mu = jnp.mean(y, axis=-1, keepdims=True)
    yc = y - mu
    var = jnp.mean(yc * yc, axis=-1, keepdims=True)
    return yc * lax.rsqrt(var + LN_EPS) * g + b


def _silu(x):
    return x * jax.nn.sigmoid(x)


def _anchor_zero(*arrays):
    folded = None
    for a in arrays:
        bits = lax.bitcast_convert_type(a, jnp.uint32)
        for r in range(0, a.shape[0], SUBLANES):
            for l in range(0, a.shape[1], LANES):
                piece = bits[r:r + SUBLANES, l:l + LANES]
                folded = piece if folded is None else folded | piece
    return lax.bitcast_convert_type((folded >> 16) >> 16, F32)


def _resident(shape, index_map):
    return pl.BlockSpec(shape, index_map, pipeline_mode=pl.Buffered(1))


def _ffn_kernel(x_ref, wg_hbm, wu_hbm, wd_hbm, g_ref, b_ref, o_ref, wg_ref, wu_ref, wd_ref, stage_g, stage_u,
                stage_d, sems, xb_ref, hid_ref, acc_ref, *permute_refs, alpha, n_tiles, layer, permute_in,
                permute_out):
    i = pl.program_id(0)
    d = acc_ref.shape[2]
    dff = wg_ref.shape[1]
    n_chunks = dff // FFN_COLS
    cur = i % 2
    prev = 1 - cur
    ln_rows = FFN_ROWS // FFN_LN_SLICES
    permute_refs = list(permute_refs)
    xin_ref, sem_x = (permute_refs.pop(0), permute_refs.pop(0)) if permute_in else (None, None)
    out_ref, sem_o = (permute_refs.pop(0), permute_refs.pop(0)) if permute_out else (None, None)

    def segment_copies(tile, slot, to_vmem):
        hbm, vmem, sem = (x_ref, xin_ref, sem_x) if to_vmem else (o_ref, out_ref, sem_o)
        copies = []
        for m in range(FFN_ROWS // TILE):
            for seg in range(SUBLANES):
                there = hbm.at[pl.ds(tile * FFN_ROWS + m * TILE + seg * SEG, SEG), :]
                here = vmem.at[slot, m * SEG:(m + 1) * SEG, seg, :]
                src, dst = (there, here) if to_vmem else (here, there)
                copies.append(pltpu.make_async_copy(src, dst, sem.at[slot, m * SUBLANES + seg]))
        return copies

    def finish_slice(k):
        rows = slice(k * ln_rows, (k + 1) * ln_rows)
        y = _layer_norm(0.5 * acc_ref[prev, rows, :], g_ref[...], b_ref[...])
        if permute_out:
            vregs = ln_rows // SUBLANES
            out_ref[prev, k * vregs:(k + 1) * vregs] = y.reshape(vregs, SUBLANES, d)
        else:
            o_ref[rows, :] = y
        return _anchor_zero(y)

    def weight_copies(c):
        slot = c % 2
        cols = pl.ds(c * FFN_COLS, FFN_COLS)
        return (pltpu.make_async_copy(wg_hbm.at[layer, :, cols], stage_g.at[slot], sems.at[slot, 0]),
                pltpu.make_async_copy(wu_hbm.at[layer, :, cols], stage_u.at[slot], sems.at[slot, 1]),
                pltpu.make_async_copy(wd_hbm.at[layer, cols, :], stage_d.at[slot], sems.at[slot, 2]))

    @pl.when(i == 0)
    def _():
        acc_ref[1] = jnp.zeros(acc_ref.shape[1:], F32)
        if permute_in:
            for cp in segment_copies(0, 0, True):
                cp.start()
        for cp in weight_copies(0):
            cp.start()
        for c in range(n_chunks):
            if c + 1 < n_chunks:
                for cp in weight_copies(c + 1):
                    cp.start()
            for cp in weight_copies(c):
                cp.wait()
            slot = c % 2
            cols = slice(c * FFN_COLS, (c + 1) * FFN_COLS)
            wg_ref[:, cols] = stage_g[slot].astype(BF16)
            wu_ref[:, cols] = stage_u[slot].astype(BF16)
            wd_ref[cols, :] = stage_d[slot].astype(BF16)

    if permute_in:
        @pl.when(i + 1 < n_tiles)
        def _():
            for cp in segment_copies(i + 1, prev, True):
                cp.start()

    if permute_out:
        @pl.when(i >= 3)
        def _():
            for cp in segment_copies(i - 3, prev, False):
                cp.wait()

    @pl.when(i < n_tiles)
    def _():
        if permute_in:
            for cp in segment_copies(i, cur, True):
                cp.wait()
            x = xin_ref[cur].reshape(FFN_ROWS, d)
        else:
            x = x_ref[...]
        xb_ref[...] = x.astype(BF16)
        anchor = None
        for c in range(n_chunks):
            cols = slice(c * FFN_COLS, (c + 1) * FFN_COLS)
            xb = xb_ref[...]
            gate = jnp.dot(xb, wg_ref[:, cols], preferred_element_type=F32)
            up = jnp.dot(xb, wu_ref[:, cols], preferred_element_type=F32)
            if anchor is not None:
                zero = jnp.concatenate([anchor] * (FFN_COLS // LANES), axis=1)
                up = jnp.concatenate([up[:SUBLANES] + zero, up[SUBLANES:]], axis=0)
            hid_ref[:, cols] = (_silu(gate) * up).astype(BF16)
            anchor = finish_slice(c) if c < FFN_LN_SLICES else None
        acc_ref[cur] = (2.0 * alpha) * x + jnp.dot(hid_ref[...], wd_ref[...], preferred_element_type=F32)

    @pl.when(i == n_tiles)
    def _():
        for k in range(FFN_LN_SLICES):
            finish_slice(k)

    if permute_out:
        @pl.when(i >= 1)
        def _():
            for cp in segment_copies(i - 1, prev, False):
                cp.start()

        @pl.when(i == n_tiles)
        def _():
            for tile in (n_tiles - 2, n_tiles - 1):
                for cp in segment_copies(tile, tile % 2, False):
                    cp.wait()


def _ffn(x2d, wg, wu, wd, g, b, layer, *, alpha, permute_in=False, permute_out=False):
    rows, d = x2d.shape
    dff = wg.shape[2]
    assert rows % FFN_ROWS == 0 and dff % FFN_COLS == 0 and FFN_ROWS % TILE == 0
    assert FFN_LN_SLICES <= dff // FFN_COLS and FFN_ROWS % FFN_LN_SLICES == 0
    n_tiles = rows // FFN_ROWS
    assert n_tiles >= 3
    vmem = (3 * d * dff * 2
            + 2 * 3 * d * FFN_COLS * 4
            + 2 * 2 * FFN_ROWS * d * 4
            + FFN_ROWS * d * (2 + 2 * 4)
            + FFN_ROWS * dff * 2
            + 3 * FFN_ROWS * FFN_COLS * 4
            + FFN_ROWS * d * 4)
    in_hbm = pl.BlockSpec(memory_space=pl.ANY)
    permute_scratch = [pltpu.VMEM((2, FFN_ROWS // SUBLANES, SUBLANES, d), F32),
                       pltpu.SemaphoreType.DMA((2, FFN_ROWS // SEG))]
    return pl.pallas_call(
        functools.partial(_ffn_kernel, alpha=alpha, n_tiles=n_tiles, layer=layer, permute_in=permute_in,
                          permute_out=permute_out),
        grid=(n_tiles + 1,),
        in_specs=[
            in_hbm if permute_in else pl.BlockSpec((FFN_ROWS, d), lambda i: (jnp.minimum(i, n_tiles - 1), 0)),
            in_hbm, in_hbm, in_hbm,
            _resident((1, d), lambda i: (0, 0)),
            _resident((1, d), lambda i: (0, 0)),
        ],
        out_specs=in_hbm if permute_out else pl.BlockSpec((FFN_ROWS, d), lambda i: (jnp.maximum(i - 1, 0), 0)),
        out_shape=jax.ShapeDtypeStruct((rows, d), F32),
        scratch_shapes=[
            pltpu.VMEM((d, dff), BF16), pltpu.VMEM((d, dff), BF16), pltpu.VMEM((dff, d), BF16),
            pltpu.VMEM((2, d, FFN_COLS), F32), pltpu.VMEM((2, d, FFN_COLS), F32),
            pltpu.VMEM((2, FFN_COLS, d), F32), pltpu.SemaphoreType.DMA((2, 3)),
            pltpu.VMEM((FFN_ROWS, d), BF16), pltpu.VMEM((FFN_ROWS, dff), BF16),
            pltpu.VMEM((2, FFN_ROWS, d), F32)]
        + (permute_scratch if permute_in else []) + (permute_scratch if permute_out else []),
        compiler_params=pltpu.CompilerParams(
            dimension_semantics=("arbitrary",),
            vmem_limit_bytes=int(vmem * 1.1)),
        name="ffn",
    )(x2d, wg, wu, wd, g.reshape(1, d), b.reshape(1, d))


def _level_plan():
    plan = []
    m = SUBLANES
    while m >= 2:
        plan.append(("seg", m))
        m //= 2
    m = SEG
    while m >= 2:
        plan.append(("pos", m))
        m //= 2
    return plan


def _level_masks():
    r = np.arange(TILE)
    seg, pos = r % SUBLANES, r // SUBLANES
    masks = []
    for kind, m in _level_plan():
        h = m // 2
        if kind == "seg":
            same = (seg[:, None] // m) == (seg[None, :] // m)
            late, early = (seg % m) >= h, (seg % m) < h
        else:
            same = (seg[:, None] == seg[None, :]) & ((pos[:, None] // m) == (pos[None, :] // m))
            late, early = (pos % m) >= h, (pos % m) < h
        masks.append(same & late[:, None] & early[None, :])
    masks = np.stack(masks)
    time = seg * SEG + pos
    causal = time[None, :] < time[:, None]
    assert (masks.sum(0) == causal).all()
    merged = [causal & ((seg[:, None] >> (LOG2_SUBLANES - d)) == (seg[None, :] >> (LOG2_SUBLANES - d)))
              for d in range(MERGE_DEPTHS)]
    for d in range(MERGE_DEPTHS):
        assert (masks[:d].sum(0) + merged[d] == causal).all()
    return np.concatenate([masks, np.stack(merged)]).astype(np.float32)


def _pool_group(ext, u_g, window, tile_in_seq):
    cur = ext[(POOL_HALO - (window - 1)) * SUBLANES:]
    span = 1
    while span < window:
        cur = cur[span * SUBLANES:] + cur[:-span * SUBLANES]
        span *= 2
    row = lax.broadcasted_iota(jnp.int32, (TILE, LANES), 0)
    pos = tile_in_seq * TILE + (row & (SUBLANES - 1)) * SEG + (row >> LOG2_SUBLANES)
    cnt = jnp.minimum(pos + 1, window).astype(F32)
    return cur / cnt - u_g


def _gates(f_raw, lb):
    t = jnp.exp(-jnp.abs(f_raw))
    log_sig = jnp.minimum(f_raw, 0.0) - jnp.log(1.0 + t)
    a = jnp.log(lb)
    c = jnp.log1p(-lb) + log_sig
    log_f = jnp.maximum(a, c) + jnp.log(1.0 + jnp.exp(-jnp.abs(a - c)))
    inv = 1.0 / (1.0 + t)
    sig_neg = jnp.where(f_raw >= 0.0, t * inv, inv)
    return log_f * LOG2E, (1.0 - lb) * sig_neg


def _vregs(a):
    return [a[j * SUBLANES:(j + 1) * SUBLANES] for j in range(a.shape[0] // SUBLANES)]


class _Head:
    def __init__(self, **kw):
        self.__dict__.update(kw)


def _seg_split_offset(off, sub, m):
    ref = off
    for blk in range(SUBLANES // m):
        src = blk * m + m // 2
        in_blk = (sub >= blk * m) & (sub < (blk + 1) * m)
        ref = jnp.where(in_blk, jnp.broadcast_to(off[src:src + 1, :], off.shape), ref)
    return off - ref


def _hgrn_prep(q, f_raw, v, lb, state_t):
    lf, kk = _gates(f_raw, lb)
    qs = q * (LANES ** -0.5)

    lf_v = _vregs(lf)
    b_v = [lf_v[0]]
    for j in range(1, SEG):
        b_v.append(b_v[-1] + lf_v[j])
    b_loc = jnp.concatenate(b_v, axis=0)
    seg_tot = b_v[-1]

    sub = lax.broadcasted_iota(jnp.int32, (SUBLANES, LANES), 0)
    inc = seg_tot
    for sh in (1, 2, 4):
        inc = inc + jnp.where(sub >= sh, pltpu.roll(inc, sh, 0), 0.0)
    off = inc - seg_tot
    total = inc[SUBLANES - 1:SUBLANES, :]
    bg = b_loc + jnp.concatenate([off] * SEG, axis=0)

    q_in = (qs * jnp.exp2(bg)).astype(BF16)
    o = lax.dot_general(q_in, state_t.astype(BF16), (((1,), (1,)), ((), ())),
                        preferred_element_type=F32)
    k_dec = (kk * jnp.exp2(total - bg)).astype(BF16)
    v_b = v.astype(BF16)
    new_state_t = state_t * jnp.exp2(total) + lax.dot_general(
        v_b, k_dec, (((0,), (0,)), ((), ())), preferred_element_type=F32)

    bounds = [off[(SUBLANES >> (d + 1)) * k:(SUBLANES >> (d + 1)) * k + 1, :] if k else jnp.zeros_like(total)
              for d in range(MERGE_DEPTHS) for k in range(2 << d)]
    span = []
    pos = 0
    for d in range(MERGE_DEPTHS):
        edges = bounds[pos:pos + (2 << d)] + [total]
        pos += 2 << d
        worst = edges[0] - edges[1]
        for k in range(1, 2 << d):
            worst = jnp.maximum(worst, edges[k] - edges[k + 1])
        span.append(worst)
    return _Head(qs=qs, kk=kk, v=v, v_b=v_b, b_v=b_v, b_loc=b_loc, off=off, sub=sub, o=o, span=span), new_state_t


def _pair_scores(head, x_q, x_k):
    return lax.dot_general((x_q * head.qs).astype(BF16), (x_k * head.kk).astype(BF16),
                           (((1,), (1,)), ((), ())), preferred_element_type=F32)


def _level_scores(head, lvl, masks_ref):
    kind, m = _level_plan()[lvl]
    h = m // 2
    if kind == "seg":
        row_sub = lax.broadcasted_iota(jnp.int32, (TILE, LANES), 0) & (SUBLANES - 1)
        t = head.b_loc + jnp.concatenate([_seg_split_offset(head.off, head.sub, m)] * SEG, axis=0)
        z = jnp.where((row_sub & (m - 1)) >= h, t, -t)
    else:
        b_v = head.b_v
        z_v = []
        for j in range(SEG):
            ref_v = b_v[(j // m) * m + h - 1]
            z_v.append(b_v[j] - ref_v if j % m >= h else ref_v - b_v[j])
        z = jnp.concatenate(z_v, axis=0)
    x = jnp.exp2(z)
    return masks_ref[lvl] * _pair_scores(head, x, x)


def _merged_scores(head, depth, masks_ref):
    t = head.b_loc + jnp.concatenate(
        [_seg_split_offset(head.off, head.sub, SUBLANES >> depth)] * SEG, axis=0)
    part = _pair_scores(head, jnp.exp2(t), jnp.exp2(-t))
    return jnp.where(masks_ref[len(_level_plan()) + depth] != 0.0, part, 0.0)


def _hgrn_scores(head, variant, masks_ref):
    n_levels = len(_level_plan())
    parts = [_level_scores(head, lvl, masks_ref) for lvl in range(min(variant, n_levels))]
    if variant < MERGE_DEPTHS:
        parts.append(_merged_scores(head, variant, masks_ref))
    scores = parts[0]
    for p in parts[1:]:
        scores = scores + p
    return scores


def _hgrn_finish(head, scores):
    o = head.o + jnp.dot(scores.astype(BF16), head.v_b, preferred_element_type=F32)
    return o + jnp.sum(head.qs * head.kk, axis=-1, keepdims=True) * head.v


def _mixer_kernel(xa_ref, xc_ref, win32_ref, poolw32_ref, pscale_ref, lb_ref, ng_ref, wout32_ref, lng_ref,
                  lnb_ref, masks_ref, o_ref, win_ref, poolw_ref, wout_ref, xab_ref, h_ref, mix_ref, pre_ref,
                  state_ref, tail_ref, *, alpha, tiles_per_seq, n_tiles):
    g = pl.program_id(0)
    last_step = n_tiles + 2

    @pl.when(g == 0)
    def _():
        win_ref[...] = win32_ref[...].astype(BF16)
        poolw_ref[...] = poolw32_ref[...].astype(BF16)
        wout_ref[...] = wout32_ref[...].astype(BF16)
        h_ref[...] = jnp.zeros_like(h_ref)
        mix_ref[...] = jnp.zeros_like(mix_ref)
        pre_ref[...] = jnp.zeros_like(pre_ref)
        state_ref[...] = jnp.zeros_like(state_ref)
        tail_ref[...] = jnp.zeros_like(tail_ref)
        h_ref[0] = jnp.dot(xa_ref[...].astype(BF16), win_ref[...], preferred_element_type=F32)

    @pl.when(g >= last_step - 1)
    def _():
        o_ref[...] = _layer_norm(pre_ref[...], lng_ref[...], lnb_ref[...])

    @pl.when(g == last_step - 1)
    def _():
        pre_ref[...] = alpha * xc_ref[...] + jnp.dot(mix_ref[(last_step - 1) % 2], wout_ref[...],
                                                     preferred_element_type=F32)

    tile_in_seq = (g + tiles_per_seq - 1) % tiles_per_seq
    first = tile_in_seq == 0

    for parity in range(2):
        pl.when((g % 2 == parity) & (g >= 1) & (g < last_step - 1))(functools.partial(
            _mixer_step, parity, 1 - parity, first, tile_in_seq, xa_ref, xc_ref, win_ref, poolw_ref,
            pscale_ref, lb_ref, ng_ref, wout_ref, lng_ref, lnb_ref, masks_ref, o_ref, xab_ref, h_ref,
            mix_ref, pre_ref, state_ref, tail_ref, alpha=alpha))


def _mixer_step(slot_a, slot_b, first, tile_in_seq, xa_ref, xc_ref, win_ref, poolw_ref, pscale_ref, lb_ref,
                ng_ref, wout_ref, lng_ref, lnb_ref, masks_ref, o_ref, xab_ref, h_ref, mix_ref, pre_ref,
                state_ref, tail_ref, *, alpha):
    d_pool = LANES * len(POOL_WINDOWS)
    d_hgrn = LANES * HGRN_HEADS
    halo_rows = POOL_HALO * SUBLANES
    d_out = o_ref.shape[1]
    o_ref[...] = _layer_norm(pre_ref[...], lng_ref[...], lnb_ref[...])
    xab_ref[...] = xa_ref[...].astype(BF16)

    def project(part):
        cols = slice(part * d_hgrn, (part + 1) * d_hgrn)
        h_ref[slot_a, :, cols] = jnp.dot(xab_ref[...], win_ref[:, cols], preferred_element_type=F32)

    def finish(part):
        cols = slice(part * d_out // HGRN_HEADS, (part + 1) * d_out // HGRN_HEADS)
        pre_ref[:, cols] = alpha * xc_ref[:, cols] + jnp.dot(mix_ref[slot_a], wout_ref[:, cols],
                                                            preferred_element_type=F32)

    project(0)
    hb = h_ref.at[slot_b]

    u = hb[:, :d_pool]
    cur_tail = u[TILE - halo_rows:, :]
    sub = lax.broadcasted_iota(jnp.int32, cur_tail.shape, 0) & (SUBLANES - 1)
    prev_tail = jnp.where(first, 0.0, tail_ref[...])
    head = jnp.where(sub == 0,
                     pltpu.roll(prev_tail, halo_rows - (SUBLANES - 1), 0),
                     pltpu.roll(cur_tail, 1, 0))
    tail_ref[...] = cur_tail
    ext = jnp.concatenate([head, u], axis=0)
    for grp, window in enumerate(POOL_WINDOWS):
        cols = slice(grp * LANES, (grp + 1) * LANES)
        pooled = _pool_group(ext[:, cols], u[:, cols], window, tile_in_seq)
        y = jnp.dot(pooled.astype(BF16), poolw_ref[grp], preferred_element_type=F32) * pscale_ref[:, cols]
        mix_ref[slot_b, :, cols] = y.astype(BF16)

    def part(hd, k):
        lo = d_pool + k * d_hgrn + hd * LANES
        return hb[:, lo:lo + LANES]

    heads = []
    for hd in range(HGRN_HEADS):
        project(hd + 1)
        state = jnp.where(first, 0.0, state_ref[hd])
        head, new_state = _hgrn_prep(part(hd, 0), part(hd, 1), part(hd, 2),
                                     lb_ref[:, hd * LANES:(hd + 1) * LANES], state)
        state_ref[hd] = new_state
        heads.append(head)

    worst = [jnp.max(functools.reduce(jnp.maximum, [hd.span[d] for hd in heads])) for d in range(MERGE_DEPTHS)]

    def mix_heads(variant):
        for hd, head in enumerate(heads):
            if variant > 0:
                finish(hd)
            o = _hgrn_finish(head, _hgrn_scores(head, variant, masks_ref))
            o = o * lax.rsqrt(jnp.mean(o * o, axis=-1, keepdims=True) + RMS_EPS) * ng_ref[...]
            mix_ref[slot_b, :, d_pool + hd * LANES:d_pool + (hd + 1) * LANES] = (
                o * _silu(part(hd, 3))).astype(BF16)
            if variant == 0:
                finish(hd)

    ok0 = worst[0] <= MERGE_LOG2_SPAN
    ok1 = worst[1] <= MERGE_LOG2_SPAN
    pl.when(ok0)(functools.partial(mix_heads, 0))
    pl.when(jnp.logical_not(ok0) & ok1)(functools.partial(mix_heads, 1))
    pl.when(jnp.logical_not(ok0 | ok1))(functools.partial(mix_heads, len(_level_plan())))


def _mixer(x3d, w_in, pool_w, pool_scale, lb, norm_g, w_out, ln_g, ln_b, masks, layer, *, alpha):
    bsz, seq, d = x3d.shape
    d_in = w_in.shape[2]
    d_mix = w_out.shape[1]
    assert seq % TILE == 0
    tiles_per_seq = seq // TILE
    n_tiles = bsz * tiles_per_seq
    n_lvl = masks.shape[0]
    x_tiles = x3d.reshape(n_tiles, TILE, d)
    last = n_tiles - 1
    vmem = (3 * 2 * TILE * d * 4
            + (4 + 2) * (d * d_in + d_mix * d + pool_w[0].size)
            + n_lvl * TILE * TILE * 4
            + 2 * TILE * d_in * 4 + 2 * TILE * d_mix * 2 + TILE * d * (2 + 4)
            + 4 * TILE * d_in * 4
            + 8 * TILE * TILE * 4)
    const = lambda *shape: _resident(shape, lambda g: (0,) * len(shape))
    at_layer = lambda *shape: _resident((None,) + shape, lambda g: (layer,) + (0,) * len(shape))
    out = pl.pallas_call(
        functools.partial(_mixer_kernel, alpha=alpha, tiles_per_seq=tiles_per_seq, n_tiles=n_tiles),
        grid=(n_tiles + 3,),
        in_specs=[
            pl.BlockSpec((None, TILE, d), lambda g: (jnp.minimum(g, last), 0, 0)),
            pl.BlockSpec((None, TILE, d), lambda g: (jnp.clip(g - 2, 0, last), 0, 0)),
            at_layer(d, d_in),
            at_layer(*pool_w.shape[1:]),
            const(1, pool_scale.shape[-1]),
            const(1, lb.shape[-1]),
            const(1, norm_g.shape[-1]),
            at_layer(d_mix, d),
            const(1, d),
            const(1, d),
            const(*masks.shape),
        ],
        out_specs=pl.BlockSpec((None, TILE, d), lambda g: (jnp.clip(g - 3, 0, last), 0, 0)),
        out_shape=jax.ShapeDtypeStruct((n_tiles, TILE, d), F32),
        scratch_shapes=[
            pltpu.VMEM((d, d_in), BF16), pltpu.VMEM(pool_w.shape[1:], BF16), pltpu.VMEM((d_mix, d), BF16),
            pltpu.VMEM((TILE, d), BF16),
            pltpu.VMEM((2, TILE, d_in), F32),
            pltpu.VMEM((2, TILE, d_mix), BF16),
            pltpu.VMEM((TILE, d), F32),
            pltpu.VMEM((HGRN_HEADS, LANES, LANES), F32),
            pltpu.VMEM((POOL_HALO * SUBLANES, LANES * len(POOL_WINDOWS)), F32),
        ],
        compiler_params=pltpu.CompilerParams(
            dimension_semantics=("arbitrary",),
            vmem_limit_bytes=int(vmem * 1.2)),
        name="mixer",
    )(x_tiles, x_tiles, w_in, pool_w, pool_scale.reshape(1, -1), lb.reshape(1, -1), norm_g.reshape(1, -1),
      w_out, ln_g.reshape(1, d), ln_b.reshape(1, d), masks)
    return out.reshape(bsz, seq, d)


def kernel(x, w_in, pool_w, pool_scale, lb_param, hgrn_norm_g, w_out, ffn1_gate, ffn1_up, ffn1_down,
           ffn2_gate, ffn2_up, ffn2_down, ln_g, ln_b):
    bsz, seq, d = x.shape
    depth = w_in.shape[0]
    alpha = (2.0 * depth) ** 0.25

    lb_all = jnp.cumsum(jax.nn.softmax(lb_param.astype(F32), axis=0), axis=0)
    lb_all = lb_all - lb_all[0:1]

    masks = jnp.asarray(_level_masks())

    x2 = x.reshape(bsz * seq, d)
    for l in range(depth):
        x2 = _ffn(x2, ffn1_gate, ffn1_up, ffn1_down, ln_g[l, 0], ln_b[l, 0], l, alpha=alpha,
                  permute_in=(l == 0))
        xm = _mixer(x2.reshape(bsz, seq, d), w_in, pool_w, pool_scale[l], lb_all[l],
                    hgrn_norm_g[l], w_out, ln_g[l, 1], ln_b[l, 1], masks, l, alpha=alpha)
        x2 = _ffn(xm.reshape(bsz * seq, d), ffn2_gate, ffn2_up, ffn2_down, ln_g[l, 2], ln_b[l, 2], l,
                  alpha=alpha, permute_out=(l == depth - 1))
    return x2.reshape(bsz, seq, d)
```

```python
import functools

import numpy as np
import jax
import jax.numpy as jnp
from jax import lax
from jax.experimental import pallas as pl
from jax.experimental.pallas import tpu as pltpu

F32 = jnp.float32
BF16 = jnp.bfloat16

SUBLANES = 8
LOG2_SUBLANES = SUBLANES.bit_length() - 1
LANES = 128
MXU_DIM = 256

POOL_WINDOWS = (2, 4, 8, 16)
HGRN_HEADS = 4
LN_EPS = 1e-5
RMS_EPS = 1e-6
LOG2E = 1.4426950408889634

TILE = 256
SEG = TILE // SUBLANES
FFN_ROWS = 512
FFN_COLS = MXU_DIM
FFN_LN_SLICES = 8
FFN_WEIGHT_SLOTS = 3
POOL_HALO = max(POOL_WINDOWS) - 1
MERGE_DEPTHS = 2
MERGE_LOG2_SPAN = 120.0


def _layer_norm(y, g, b):
    mu = jnp.mean(y, axis=-1, keepdims=True)
    yc = y - mu
    var = jnp.mean(yc * yc, axis=-1, keepdims=True)
    return yc * lax.rsqrt(var + LN_EPS) * g + b


def _silu(x):
    return x * jax.nn.sigmoid(x)


def _anchor_zero(*arrays):
    folded = None
    for a in arrays:
        bits = lax.bitcast_convert_type(a, jnp.uint32)
        for r in range(0, a.shape[0], SUBLANES):
            for l in range(0, a.shape[1], LANES):
                piece = bits[r:r + SUBLANES, l:l + LANES]
                folded = piece if folded is None else folded | piece
    return lax.bitcast_convert_type((folded >> 16) >> 16, F32)


def _resident(shape, index_map):
    return pl.BlockSpec(shape, index_map, pipeline_mode=pl.Buffered(1))


def _ffn_kernel(x_ref, wg_hbm, wu_hbm, wd_hbm, g_ref, b_ref, o_ref, wg_ref, wu_ref, wd_ref, stage_g, stage_u,
                stage_d, sems, xb_ref, hid_ref, acc_ref, *permute_refs, alpha, n_tiles, layer, permute_in,
                permute_out):
    i = pl.program_id(0)
    d = acc_ref.shape[2]
    dff = wg_ref.shape[1]
    n_chunks = dff // FFN_COLS
    cur = i % 2
    prev = 1 - cur
    ln_rows = FFN_ROWS // FFN_LN_SLICES
    permute_refs = list(permute_refs)
    xin_ref, sem_x = (permute_refs.pop(0), permute_refs.pop(0)) if permute_in else (None, None)
    out_ref, sem_o = (permute_refs.pop(0), permute_refs.pop(0)) if permute_out else (None, None)

    def segment_copies(tile, slot, to_vmem):
        hbm, vmem, sem = (x_ref, xin_ref, sem_x) if to_vmem else (o_ref, out_ref, sem_o)
        copies = []
        for m in range(FFN_ROWS // TILE):
            for seg in range(SUBLANES):
                there = hbm.at[pl.ds(tile * FFN_ROWS + m * TILE + seg * SEG, SEG), :]
                here = vmem.at[slot, m * SEG:(m + 1) * SEG, seg, :]
                src, dst = (there, here) if to_vmem else (here, there)
                copies.append(pltpu.make_async_copy(src, dst, sem.at[slot, m * SUBLANES + seg]))
        return copies

    def finish_slice(k):
        rows = slice(k * ln_rows, (k + 1) * ln_rows)
        y = _layer_norm(0.5 * acc_ref[prev, rows, :], g_ref[...], b_ref[...])
        if permute_out:
            vregs = ln_rows // SUBLANES
            out_ref[prev, k * vregs:(k + 1) * vregs] = y.reshape(vregs, SUBLANES, d)
        else:
            o_ref[rows, :] = y
        return _anchor_zero(y)

    def weight_copies(c):
        slot = c % FFN_WEIGHT_SLOTS
        cols = pl.ds(c * FFN_COLS, FFN_COLS)
        return (pltpu.make_async_copy(wg_hbm.at[layer, :, cols], stage_g.at[slot], sems.at[slot, 0]),
                pltpu.make_async_copy(wu_hbm.at[layer, :, cols], stage_u.at[slot], sems.at[slot, 1]),
                pltpu.make_async_copy(wd_hbm.at[layer, cols, :], stage_d.at[slot], sems.at[slot, 2]))

    @pl.when(i == 0)
    def _():
        acc_ref[1] = jnp.zeros(acc_ref.shape[1:], F32)
        if permute_in:
            for cp in segment_copies(0, 0, True):
                cp.start()
        ahead = FFN_WEIGHT_SLOTS - 1
        for c in range(min(ahead, n_chunks)):
            for cp in weight_copies(c):
                cp.start()
        for c in range(n_chunks):
            if c + ahead < n_chunks:
                for cp in weight_copies(c + ahead):
                    cp.start()
            for cp in weight_copies(c):
                cp.wait()
            slot = c % FFN_WEIGHT_SLOTS
            cols = slice(c * FFN_COLS, (c + 1) * FFN_COLS)
            wg_ref[:, cols] = stage_g[slot].astype(BF16)
            wu_ref[:, cols] = stage_u[slot].astype(BF16)
            wd_ref[cols, :] = stage_d[slot].astype(BF16)

    if permute_in:
        @pl.when(i + 1 < n_tiles)
        def _():
            for cp in segment_copies(i + 1, prev, True):
                cp.start()

    if permute_out:
        @pl.when(i >= 3)
        def _():
            for cp in segment_copies(i - 3, prev, False):
                cp.wait()

    @pl.when(i < n_tiles)
    def _():
        if permute_in:
            for cp in segment_copies(i, cur, True):
                cp.wait()
            x = xin_ref[cur].reshape(FFN_ROWS, d)
        else:
            x = x_ref[...]
        xb_ref[...] = x.astype(BF16)
        anchor = None
        for c in range(n_chunks):
            cols = slice(c * FFN_COLS, (c + 1) * FFN_COLS)
            xb = xb_ref[...]
            gate = jnp.dot(xb, wg_ref[:, cols], preferred_element_type=F32)
            up = jnp.dot(xb, wu_ref[:, cols], preferred_element_type=F32)
            if anchor is not None:
                zero = jnp.concatenate([anchor] * (FFN_COLS // LANES), axis=1)
                up = jnp.concatenate([up[:SUBLANES] + zero, up[SUBLANES:]], axis=0)
            hid_ref[:, cols] = (_silu(gate) * up).astype(BF16)
            anchor = finish_slice(c) if c < FFN_LN_SLICES else None
        acc_ref[cur] = (2.0 * alpha) * x + jnp.dot(hid_ref[...], wd_ref[...], preferred_element_type=F32)

    @pl.when(i == n_tiles)
    def _():
        for k in range(FFN_LN_SLICES):
            finish_slice(k)

    if permute_out:
        @pl.when(i >= 1)
        def _():
            for cp in segment_copies(i - 1, prev, False):
                cp.start()

        @pl.when(i == n_tiles)
        def _():
            for tile in (n_tiles - 2, n_tiles - 1):
                for cp in segment_copies(tile, tile % 2, False):
                    cp.wait()


def _ffn(x2d, wg, wu, wd, g, b, layer, *, alpha, permute_in=False, permute_out=False):
    rows, d = x2d.shape
    dff = wg.shape[2]
    assert rows % FFN_ROWS == 0 and dff % FFN_COLS == 0 and FFN_ROWS % TILE == 0
    assert FFN_LN_SLICES <= dff // FFN_COLS and FFN_ROWS % FFN_LN_SLICES == 0
    n_tiles = rows // FFN_ROWS
    assert n_tiles >= 3
    vmem = (3 * d * dff * 2
            + FFN_WEIGHT_SLOTS * 3 * d * FFN_COLS * 4
            + 2 * 2 * FFN_ROWS * d * 4
            + FFN_ROWS * d * (2 + 2 * 4)
            + FFN_ROWS * dff * 2
            + 3 * FFN_ROWS * FFN_COLS * 4
            + FFN_ROWS * d * 4)
    in_hbm = pl.BlockSpec(memory_space=pl.ANY)
    permute_scratch = [pltpu.VMEM((2, FFN_ROWS // SUBLANES, SUBLANES, d), F32),
                       pltpu.SemaphoreType.DMA((2, FFN_ROWS // SEG))]
    return pl.pallas_call(
        functools.partial(_ffn_kernel, alpha=alpha, n_tiles=n_tiles, layer=layer, permute_in=permute_in,
                          permute_out=permute_out),
        grid=(n_tiles + 1,),
        in_specs=[
            in_hbm if permute_in else pl.BlockSpec((FFN_ROWS, d), lambda i: (jnp.minimum(i, n_tiles - 1), 0)),
            in_hbm, in_hbm, in_hbm,
            _resident((1, d), lambda i: (0, 0)),
            _resident((1, d), lambda i: (0, 0)),
        ],
        out_specs=in_hbm if permute_out else pl.BlockSpec((FFN_ROWS, d), lambda i: (jnp.maximum(i - 1, 0), 0)),
        out_shape=jax.ShapeDtypeStruct((rows, d), F32),
        scratch_shapes=[
            pltpu.VMEM((d, dff), BF16), pltpu.VMEM((d, dff), BF16), pltpu.VMEM((dff, d), BF16),
            pltpu.VMEM((FFN_WEIGHT_SLOTS, d, FFN_COLS), F32), pltpu.VMEM((FFN_WEIGHT_SLOTS, d, FFN_COLS), F32),
            pltpu.VMEM((FFN_WEIGHT_SLOTS, FFN_COLS, d), F32), pltpu.SemaphoreType.DMA((FFN_WEIGHT_SLOTS, 3)),
            pltpu.VMEM((FFN_ROWS, d), BF16), pltpu.VMEM((FFN_ROWS, dff), BF16),
            pltpu.VMEM((2, FFN_ROWS, d), F32)]
        + (permute_scratch if permute_in else []) + (permute_scratch if permute_out else []),
        compiler_params=pltpu.CompilerParams(
            dimension_semantics=("arbitrary",),
            vmem_limit_bytes=int(vmem * 1.1)),
        name="ffn",
    )(x2d, wg, wu, wd, g.reshape(1, d), b.reshape(1, d))


def _level_plan():
    plan = []
    m = SUBLANES
    while m >= 2:
        plan.append(("seg", m))
        m //= 2
    m = SEG
    while m >= 2:
        plan.append(("pos", m))
        m //= 2
    return plan


def _level_masks():
    r = np.arange(TILE)
    seg, pos = r % SUBLANES, r // SUBLANES
    masks = []
    for kind, m in _level_plan():
        h = m // 2
        if kind == "seg":
            same = (seg[:, None] // m) == (seg[None, :] // m)
            late, early = (seg % m) >= h, (seg % m) < h
        else:
            same = (seg[:, None] == seg[None, :]) & ((pos[:, None] // m) == (pos[None, :] // m))
            late, early = (pos % m) >= h, (pos % m) < h
        masks.append(same & late[:, None] & early[None, :])
    masks = np.stack(masks)
    time = seg * SEG + pos
    causal = time[None, :] < time[:, None]
    assert (masks.sum(0) == causal).all()
    merged = [causal & ((seg[:, None] >> (LOG2_SUBLANES - d)) == (seg[None, :] >> (LOG2_SUBLANES - d)))
              for d in range(MERGE_DEPTHS)]
    for d in range(MERGE_DEPTHS):
        assert (masks[:d].sum(0) + merged[d] == causal).all()
    return np.concatenate([masks, np.stack(merged)]).astype(np.float32)


def _pool_group(ext, u_g, window, tile_in_seq):
    cur = ext[(POOL_HALO - (window - 1)) * SUBLANES:]
    span = 1
    while span < window:
        cur = cur[span * SUBLANES:] + cur[:-span * SUBLANES]
        span *= 2
    row = lax.broadcasted_iota(jnp.int32, (TILE, LANES), 0)
    pos = tile_in_seq * TILE + (row & (SUBLANES - 1)) * SEG + (row >> LOG2_SUBLANES)
    cnt = jnp.minimum(pos + 1, window).astype(F32)
    return cur / cnt - u_g


def _gates(f_raw, lb):
    t = jnp.exp(-jnp.abs(f_raw))
    log_sig = jnp.minimum(f_raw, 0.0) - jnp.log(1.0 + t)
    a = jnp.log(lb)
    c = jnp.log1p(-lb) + log_sig
    log_f = jnp.maximum(a, c) + jnp.log(1.0 + jnp.exp(-jnp.abs(a - c)))
    inv = 1.0 / (1.0 + t)
    sig_neg = jnp.where(f_raw >= 0.0, t * inv, inv)
    return log_f * LOG2E, (1.0 - lb) * sig_neg


def _vregs(a):
    return [a[j * SUBLANES:(j + 1) * SUBLANES] for j in range(a.shape[0] // SUBLANES)]


class _Head:
    def __init__(self, **kw):
        self.__dict__.update(kw)


def _seg_split_offset(off, sub, m):
    ref = off
    for blk in range(SUBLANES // m):
        src = blk * m + m // 2
        in_blk = (sub >= blk * m) & (sub < (blk + 1) * m)
        ref = jnp.where(in_blk, jnp.broadcast_to(off[src:src + 1, :], off.shape), ref)
    return off - ref


def _hgrn_prep(q, f_raw, v, lb, state_t):
    lf, kk = _gates(f_raw, lb)
    qs = q * (LANES ** -0.5)

    lf_v = _vregs(lf)
    b_v = [lf_v[0]]
    for j in range(1, SEG):
        b_v.append(b_v[-1] + lf_v[j])
    b_loc = jnp.concatenate(b_v, axis=0)
    seg_tot = b_v[-1]

    sub = lax.broadcasted_iota(jnp.int32, (SUBLANES, LANES), 0)
    inc = seg_tot
    for sh in (1, 2, 4):
        inc = inc + jnp.where(sub >= sh, pltpu.roll(inc, sh, 0), 0.0)
    off = inc - seg_tot
    total = inc[SUBLANES - 1:SUBLANES, :]
    bg = b_loc + jnp.concatenate([off] * SEG, axis=0)

    q_in = (qs * jnp.exp2(bg)).astype(BF16)
    o = lax.dot_general(q_in, state_t.astype(BF16), (((1,), (1,)), ((), ())),
                        preferred_element_type=F32)
    k_dec = (kk * jnp.exp2(total - bg)).astype(BF16)
    v_b = v.astype(BF16)
    new_state_t = state_t * jnp.exp2(total) + lax.dot_general(
        v_b, k_dec, (((0,), (0,)), ((), ())), preferred_element_type=F32)

    bounds = [off[(SUBLANES >> (d + 1)) * k:(SUBLANES >> (d + 1)) * k + 1, :] if k else jnp.zeros_like(total)
              for d in range(MERGE_DEPTHS) for k in range(2 << d)]
    span = []
    pos = 0
    for d in range(MERGE_DEPTHS):
        edges = bounds[pos:pos + (2 << d)] + [total]
        pos += 2 << d
        worst = edges[0] - edges[1]
        for k in range(1, 2 << d):
            worst = jnp.maximum(worst, edges[k] - edges[k + 1])
        span.append(worst)
    return _Head(qs=qs, kk=kk, v=v, v_b=v_b, b_v=b_v, b_loc=b_loc, off=off, sub=sub, o=o, span=span), new_state_t


def _pair_scores(head, x_q, x_k):
    return lax.dot_general((x_q * head.qs).astype(BF16), (x_k * head.kk).astype(BF16),
                           (((1,), (1,)), ((), ())), preferred_element_type=F32)


def _level_scores(head, lvl, masks_ref):
    kind, m = _level_plan()[lvl]
    h = m // 2
    if kind == "seg":
        row_sub = lax.broadcasted_iota(jnp.int32, (TILE, LANES), 0) & (SUBLANES - 1)
        t = head.b_loc + jnp.concatenate([_seg_split_offset(head.off, head.sub, m)] * SEG, axis=0)
        z = jnp.where((row_sub & (m - 1)) >= h, t, -t)
    else:
        b_v = head.b_v
        z_v = []
        for j in range(SEG):
            ref_v = b_v[(j // m) * m + h - 1]
            z_v.append(b_v[j] - ref_v if j % m >= h else ref_v - b_v[j])
        z = jnp.concatenate(z_v, axis=0)
    x = jnp.exp2(z)
    return masks_ref[lvl] * _pair_scores(head, x, x)


def _merged_scores(head, depth, masks_ref):
    t = head.b_loc + jnp.concatenate(
        [_seg_split_offset(head.off, head.sub, SUBLANES >> depth)] * SEG, axis=0)
    part = _pair_scores(head, jnp.exp2(t), jnp.exp2(-t))
    return jnp.where(masks_ref[len(_level_plan()) + depth] != 0.0, part, 0.0)


def _hgrn_scores(head, variant, masks_ref):
    n_levels = len(_level_plan())
    parts = [_level_scores(head, lvl, masks_ref) for lvl in range(min(variant, n_levels))]
    if variant < MERGE_DEPTHS:
        parts.append(_merged_scores(head, variant, masks_ref))
    scores = parts[0]
    for p in parts[1:]:
        scores = scores + p
    return scores


def _hgrn_finish(head, scores):
    o = head.o + jnp.dot(scores.astype(BF16), head.v_b, preferred_element_type=F32)
    return o + jnp.sum(head.qs * head.kk, axis=-1, keepdims=True) * head.v


def _mixer_kernel(xa_ref, xc_ref, win32_ref, poolw32_ref, pscale_ref, lb_ref, ng_ref, wout32_ref, lng_ref,
                  lnb_ref, masks_ref, o_ref, win_ref, poolw_ref, wout_ref, xab_ref, h_ref, mix_ref, pre_ref,
                  state_ref, tail_ref, *, alpha, tiles_per_seq, n_tiles):
    g = pl.program_id(0)
    last_step = n_tiles + 2

    @pl.when(g == 0)
    def _():
        win_ref[...] = win32_ref[...].astype(BF16)
        poolw_ref[...] = poolw32_ref[...].astype(BF16)
        wout_ref[...] = wout32_ref[...].astype(BF16)
        h_ref[...] = jnp.zeros_like(h_ref)
        mix_ref[...] = jnp.zeros_like(mix_ref)
        pre_ref[...] = jnp.zeros_like(pre_ref)
        state_ref[...] = jnp.zeros_like(state_ref)
        tail_ref[...] = jnp.zeros_like(tail_ref)
        h_ref[0] = jnp.dot(xa_ref[...].astype(BF16), win_ref[...], preferred_element_type=F32)

    @pl.when(g >= last_step - 1)
    def _():
        o_ref[...] = _layer_norm(pre_ref[...], lng_ref[...], lnb_ref[...])

    @pl.when(g == last_step - 1)
    def _():
        pre_ref[...] = alpha * xc_ref[...] + jnp.dot(mix_ref[(last_step - 1) % 2], wout_ref[...],
                                                     preferred_element_type=F32)

    tile_in_seq = (g + tiles_per_seq - 1) % tiles_per_seq
    first = tile_in_seq == 0

    for parity in range(2):
        pl.when((g % 2 == parity) & (g >= 1) & (g < last_step - 1))(functools.partial(
            _mixer_step, parity, 1 - parity, first, tile_in_seq, xa_ref, xc_ref, win_ref, poolw_ref,
            pscale_ref, lb_ref, ng_ref, wout_ref, lng_ref, lnb_ref, masks_ref, o_ref, xab_ref, h_ref,
            mix_ref, pre_ref, state_ref, tail_ref, alpha=alpha))


def _mixer_step(slot_a, slot_b, first, tile_in_seq, xa_ref, xc_ref, win_ref, poolw_ref, pscale_ref, lb_ref,
                ng_ref, wout_ref, lng_ref, lnb_ref, masks_ref, o_ref, xab_ref, h_ref, mix_ref, pre_ref,
                state_ref, tail_ref, *, alpha):
    d_pool = LANES * len(POOL_WINDOWS)
    d_hgrn = LANES * HGRN_HEADS
    halo_rows = POOL_HALO * SUBLANES
    d_out = o_ref.shape[1]
    o_ref[...] = _layer_norm(pre_ref[...], lng_ref[...], lnb_ref[...])
    xab_ref[...] = xa_ref[...].astype(BF16)

    def project(part):
        cols = slice(part * d_hgrn, (part + 1) * d_hgrn)
        h_ref[slot_a, :, cols] = jnp.dot(xab_ref[...], win_ref[:, cols], preferred_element_type=F32)

    def finish(part):
        cols = slice(part * d_out // HGRN_HEADS, (part + 1) * d_out // HGRN_HEADS)
        pre_ref[:, cols] = alpha * xc_ref[:, cols] + jnp.dot(mix_ref[slot_a], wout_ref[:, cols],
                                                            preferred_element_type=F32)

    project(0)
    hb = h_ref.at[slot_b]

    u = hb[:, :d_pool]
    cur_tail = u[TILE - halo_rows:, :]
    sub = lax.broadcasted_iota(jnp.int32, cur_tail.shape, 0) & (SUBLANES - 1)
    prev_tail = jnp.where(first, 0.0, tail_ref[...])
    head = jnp.where(sub == 0,
                     pltpu.roll(prev_tail, halo_rows - (SUBLANES - 1), 0),
                     pltpu.roll(cur_tail, 1, 0))
    tail_ref[...] = cur_tail
    ext = jnp.concatenate([head, u], axis=0)
    for grp, window in enumerate(POOL_WINDOWS):
        cols = slice(grp * LANES, (grp + 1) * LANES)
        pooled = _pool_group(ext[:, cols], u[:, cols], window, tile_in_seq)
        y = jnp.dot(pooled.astype(BF16), poolw_ref[grp], preferred_element_type=F32) * pscale_ref[:, cols]
        mix_ref[slot_b, :, cols] = y.astype(BF16)

    def part(hd, k):
        lo = d_pool + k * d_hgrn + hd * LANES
        return hb[:, lo:lo + LANES]

    heads = []
    for hd in range(HGRN_HEADS):
        project(hd + 1)
        state = jnp.where(first, 0.0, state_ref[hd])
        head, new_state = _hgrn_prep(part(hd, 0), part(hd, 1), part(hd, 2),
                                     lb_ref[:, hd * LANES:(hd + 1) * LANES], state)
        state_ref[hd] = new_state
        heads.append(head)

    worst = [jnp.max(functools.reduce(jnp.maximum, [hd.span[d] for hd in heads])) for d in range(MERGE_DEPTHS)]

    def mix_heads(variant):
        for hd, head in enumerate(heads):
            if variant > 0:
                finish(hd)
            o = _hgrn_finish(head, _hgrn_scores(head, variant, masks_ref))
            o = o * lax.rsqrt(jnp.mean(o * o, axis=-1, keepdims=True) + RMS_EPS) * ng_ref[...]
            mix_ref[slot_b, :, d_pool + hd * LANES:d_pool + (hd + 1) * LANES] = (
                o * _silu(part(hd, 3))).astype(BF16)
            if variant == 0:
                finish(hd)

    ok0 = worst[0] <= MERGE_LOG2_SPAN
    ok1 = worst[1] <= MERGE_LOG2_SPAN
    pl.when(ok0)(functools.partial(mix_heads, 0))
    pl.when(jnp.logical_not(ok0) & ok1)(functools.partial(mix_heads, 1))
    pl.when(jnp.logical_not(ok0 | ok1))(functools.partial(mix_heads, len(_level_plan())))


def _mixer(x3d, w_in, pool_w, pool_scale, lb, norm_g, w_out, ln_g, ln_b, masks, layer, *, alpha):
    bsz, seq, d = x3d.shape
    d_in = w_in.shape[2]
    d_mix = w_out.shape[1]
    assert seq % TILE == 0
    tiles_per_seq = seq // TILE
    n_tiles = bsz * tiles_per_seq
    n_lvl = masks.shape[0]
    x_tiles = x3d.reshape(n_tiles, TILE, d)
    last = n_tiles - 1
    vmem = (3 * 2 * TILE * d * 4
            + (4 + 2) * (d * d_in + d_mix * d + pool_w[0].size)
            + n_lvl * TILE * TILE * 4
            + 2 * TILE * d_in * 4 + 2 * TILE * d_mix * 2 + TILE * d * (2 + 4)
            + 4 * TILE * d_in * 4
            + 8 * TILE * TILE * 4)
    const = lambda *shape: _resident(shape, lambda g: (0,) * len(shape))
    at_layer = lambda *shape: _resident((None,) + shape, lambda g: (layer,) + (0,) * len(shape))
    out = pl.pallas_call(
        functools.partial(_mixer_kernel, alpha=alpha, tiles_per_seq=tiles_per_seq, n_tiles=n_tiles),
        grid=(n_tiles + 3,),
        in_specs=[
            pl.BlockSpec((None, TILE, d), lambda g: (jnp.minimum(g, last), 0, 0)),
            pl.BlockSpec((None, TILE, d), lambda g: (jnp.clip(g - 2, 0, last), 0, 0)),
            at_layer(d, d_in),
            at_layer(*pool_w.shape[1:]),
            const(1, pool_scale.shape[-1]),
            const(1, lb.shape[-1]),
            const(1, norm_g.shape[-1]),
            at_layer(d_mix, d),
            const(1, d),
            const(1, d),
            const(*masks.shape),
        ],
        out_specs=pl.BlockSpec((None, TILE, d), lambda g: (jnp.clip(g - 3, 0, last), 0, 0)),
        out_shape=jax.ShapeDtypeStruct((n_tiles, TILE, d), F32),
        scratch_shapes=[
            pltpu.VMEM((d, d_in), BF16), pltpu.VMEM(pool_w.shape[1:], BF16), pltpu.VMEM((d_mix, d), BF16),
            pltpu.VMEM((TILE, d), BF16),
            pltpu.VMEM((2, TILE, d_in), F32),
            pltpu.VMEM((2, TILE, d_mix), BF16),
            pltpu.VMEM((TILE, d), F32),
            pltpu.VMEM((HGRN_HEADS, LANES, LANES), F32),
            pltpu.VMEM((POOL_HALO * SUBLANES, LANES * len(POOL_WINDOWS)), F32),
        ],
        compiler_params=pltpu.CompilerParams(
            dimension_semantics=("arbitrary",),
            vmem_limit_bytes=int(vmem * 1.2)),
        name="mixer",
    )(x_tiles, x_tiles, w_in, pool_w, pool_scale.reshape(1, -1), lb.reshape(1, -1), norm_g.reshape(1, -1),
      w_out, ln_g.reshape(1, d), ln_b.reshape(1, d), masks)
    return out.reshape(bsz, seq, d)


def kernel(x, w_in, pool_w, pool_scale, lb_param, hgrn_norm_g, w_out, ffn1_gate, ffn1_up, ffn1_down,
           ffn2_gate, ffn2_up, ffn2_down, ln_g, ln_b):
    bsz, seq, d = x.shape
    depth = w_in.shape[0]
    alpha = (2.0 * depth) ** 0.25

    lb_all = jnp.cumsum(jax.nn.softmax(lb_param.astype(F32), axis=0), axis=0)
    lb_all = lb_all - lb_all[0:1]

    masks = jnp.asarray(_level_masks())

    x2 = x.reshape(bsz * seq, d)
    for l in range(depth):
        x2 = _ffn(x2, ffn1_gate, ffn1_up, ffn1_down, ln_g[l, 0], ln_b[l, 0], l, alpha=alpha,
                  permute_in=(l == 0))
        xm = _mixer(x2.reshape(bsz, seq, d), w_in, pool_w, pool_scale[l], lb_all[l],
                    hgrn_norm_g[l], w_out, ln_g[l, 1], ln_b[l, 1], masks, l, alpha=alpha)
        x2 = _ffn(xm.reshape(bsz * seq, d), ffn2_gate, ffn2_up, ffn2_down, ln_g[l, 2], ln_b[l, 2], l,
                  alpha=alpha, permute_out=(l == depth - 1))
    return x2.reshape(bsz, seq, d)
```
